```python
import jax, jax.numpy as jnp
from jax import lax
import numpy as np

D_MODEL = 1024
BATCH = 8
SEQ = 8192
DEPTH = 1
DEC_BATCH = 8
DEC_SEQ = 16
PAST_LEN = 4096

CHUNK = 64
D_CONV = 1024
CONV_WIDTH = 31
CONV_STATE = CONV_WIDTH - 1
N_HEADS = 4
KEY_DIM = D_MODEL // 2
VAL_DIM = D_MODEL
HEAD_K = KEY_DIM // N_HEADS
HEAD_V = VAL_DIM // N_HEADS
GATE_RANK = 16
GATE_NORM = 16.0
N_GROUPS = 4
EXPERTS_PER_GROUP = 8
N_EXPERTS = N_GROUPS * EXPERTS_PER_GROUP
TOP_K = 2
D_EXPERT = D_MODEL // 2
PLE_DIM = 256
LN_EPS = 1e-5
RMS_EPS = 1e-6
ALPHA = (2.0 * DEPTH) ** 0.25
BETA = (8.0 * DEPTH) ** -0.25
N_IN = 2 * D_CONV + 2 * KEY_DIM + 2 * VAL_DIM + 2 * D_MODEL + GATE_RANK

kernel_name = "hybrid_conv_gla_hiermoe_stream_step"


def layer_norm(x, g, b):
    xf = x.astype(jnp.float32)
    mu = xf.mean(-1, keepdims=True)
    var = jnp.square(xf - mu).mean(-1, keepdims=True)
    y = (xf - mu) * lax.rsqrt(var + LN_EPS) * g.astype(jnp.float32) + b.astype(jnp.float32)
    return y.astype(x.dtype)


def conv_branch(u, conv_state, conv_w, conv_b, ln_g, ln_b, w_out):
    buf = jnp.concatenate([conv_state.astype(u.dtype), u], axis=1)
    h = lax.conv_general_dilated(buf, conv_w[:, None, :], window_strides=(1,), padding='VALID',
                                 dimension_numbers=('NWC', 'WIO', 'NWC'),
                                 feature_group_count=D_CONV) + conv_b
    h = jax.nn.silu(layer_norm(h, ln_g, ln_b))
    return h @ w_out, buf[:, -CONV_STATE:]


def gla_scan(q, k, v, log_a, s0):
    B, L = q.shape[0], q.shape[1]
    C = min(CHUNK, L)
    N = L // C
    r = lambda t: t.reshape(B, N, C, *t.shape[2:])
    q, k, v, log_a = r(q), r(k), r(v), r(log_a)
    b = jnp.cumsum(log_a, axis=2)
    b_ref = b[:, :, C // 2:C // 2 + 1]
    b_last = b[:, :, C - 1:C]
    scores = jnp.einsum('bnthk,bnshk->bnhts', q * jnp.exp(b - b_ref), k * jnp.exp(b_ref - b))
    causal = jnp.tril(jnp.ones((C, C), dtype=bool))
    scores = jnp.where(causal, scores, 0.0)
    o_intra = jnp.einsum('bnhts,bnshv->bnthv', scores, v)
    q_in = q * jnp.exp(b)
    k_out = k * jnp.exp(b_last - b)
    decay = jnp.exp(b_last[:, :, 0])

    def step(s, xs):
        qc, kc, vc, dc = xs
        o = jnp.einsum('bthk,bhkv->bthv', qc, s)
        s = s * dc[..., None] + jnp.einsum('bshk,bshv->bhkv', kc, vc)
        return s, o

    sw = lambda t: jnp.moveaxis(t, 1, 0)
    s_final, o_inter = lax.scan(step, s0, (sw(q_in), sw(k_out), sw(v), sw(decay)))
    o = o_intra + jnp.moveaxis(o_inter, 0, 1)
    return o.reshape(B, L, N_HEADS, HEAD_V), s_final


def hier_moe(x, w_rg, b_rg, w_re, b_re, w_e_gate, w_e_up, w_e_down):
    B, L, D = x.shape
    t = x.reshape(-1, D)
    T = t.shape[0]
    g_logits = (t @ w_rg).astype(jnp.float32) + b_rg.astype(jnp.float32)
    g_prob = jax.nn.softmax(g_logits, axis=-1)
    grp = jnp.argmax(g_logits, axis=-1)
    g_w = jnp.take_along_axis(g_prob, grp[:, None], axis=-1)
    e_all = ((t @ w_re).astype(jnp.float32) + b_re.astype(jnp.float32)).reshape(T, N_GROUPS, EXPERTS_PER_GROUP)
    e_logits = jnp.take_along_axis(e_all, grp[:, None, None], axis=1)[:, 0]
    top_v, top_i = lax.top_k(e_logits, TOP_K)
    weight = g_w * jax.nn.softmax(top_v, axis=-1)
    expert_id = (grp[:, None] * EXPERTS_PER_GROUP + top_i).reshape(-1)
    order = jnp.argsort(expert_id)
    xs = t[order // TOP_K]
    sizes = jnp.bincount(expert_id, length=N_EXPERTS).astype(jnp.int32)
    hg = lax.ragged_dot(xs, w_e_gate, sizes)
    hu = lax.ragged_dot(xs, w_e_up, sizes)
    ys = lax.ragged_dot(jax.nn.silu(hg) * hu, w_e_down, sizes)
    y = jnp.zeros_like(ys).at[order].set(ys).reshape(T, TOP_K, D)
    out = jnp.einsum('tkd,tk->td', y, weight.astype(y.dtype))
    return out.reshape(B, L, D)


def encoder_layer(h, p, conv_state, gla_state, w_in, w_gate_up, b_gate, conv_w, conv_b, conv_ln_g, conv_ln_b,
                  w_conv_out, gla_norm_g, w_o, ln1_g, ln1_b, w_rg, b_rg, w_re, b_re, w_e_gate, w_e_up,
                  w_e_down, ln2_g, ln2_b, w_ple, w_ple_gate):
    B, L, _ = h.shape
    proj = h @ w_in
    sizes = [D_CONV, D_CONV, KEY_DIM, KEY_DIM, VAL_DIM, VAL_DIM, D_MODEL, D_MODEL, GATE_RANK]
    cv_a, cv_b, q, k, v, g_out, gate_a, gate_b, z = jnp.split(proj, np.cumsum(sizes)[:-1].tolist(), axis=-1)
    u = cv_a * jax.nn.sigmoid(cv_b)
    a_out, conv_new = conv_branch(u, conv_state, conv_w, conv_b, conv_ln_g, conv_ln_b, w_conv_out)
    f32 = jnp.float32
    log_a = jax.nn.log_sigmoid((z @ w_gate_up + b_gate).astype(f32)) / GATE_NORM
    hd = lambda t, d: t.astype(f32).reshape(B, L, N_HEADS, d)
    o, s_new = gla_scan(hd(q, HEAD_K) * (HEAD_K ** -0.5), hd(k, HEAD_K), hd(v, HEAD_V),
                        log_a.reshape(B, L, N_HEADS, HEAD_K), gla_state.astype(f32))
    o = o * lax.rsqrt(jnp.mean(o * o, axis=-1, keepdims=True) + RMS_EPS) * gla_norm_g.astype(f32)
    b_out = o.reshape(B, L, VAL_DIM).astype(h.dtype) * jax.nn.silu(g_out)
    merged = jax.nn.sigmoid(gate_a) * a_out + jax.nn.sigmoid(gate_b) * b_out
    h = layer_norm(ALPHA * h + merged @ w_o, ln1_g, ln1_b)
    h = layer_norm(ALPHA * h + hier_moe(h, w_rg, b_rg, w_re, b_re, w_e_gate, w_e_up, w_e_down), ln2_g, ln2_b)
    h = h + jax.nn.sigmoid(h @ w_ple_gate) * (p @ w_ple)
    return h, conv_new, s_new.astype(h.dtype)


def setup_inputs(seed: int = 0) -> dict:
    key = jax.random.key(seed)
    ks = jax.random.split(key, 40)
    n = lambda i, shape, s: jax.random.normal(ks[i], shape, jnp.float32) * s
    Dp = DEPTH
    return {
        "x_prompt": n(0, (BATCH, SEQ, D_MODEL), 1.0),
        "x_sample": n(1, (DEC_BATCH, DEC_SEQ, D_MODEL), 1.0),
        "p_prompt": n(2, (DEPTH, BATCH, SEQ, PLE_DIM), 1.0),
        "p_sample": n(3, (DEPTH, DEC_BATCH, DEC_SEQ, PLE_DIM), 1.0),
        "state_conv": n(4, (DEPTH, DEC_BATCH, CONV_STATE, D_CONV), 0.5),
        "state_gla": n(5, (DEPTH, DEC_BATCH, N_HEADS, HEAD_K, HEAD_V), 0.1),
        "ln_in_g": 1.0 + n(6, (D_MODEL,), 0.02),
        "ln_in_b": n(7, (D_MODEL,), 0.02),
        "w_in": n(8, (Dp, D_MODEL, N_IN), D_MODEL ** -0.5),
        "w_gate_up": n(9, (Dp, GATE_RANK, KEY_DIM), GATE_RANK ** -0.5),
        "b_gate": n(10, (Dp, KEY_DIM), 0.1),
        "conv_w": n(11, (Dp, CONV_WIDTH, D_CONV), CONV_WIDTH ** -0.5),
        "conv_b": n(12, (Dp, D_CONV), 0.02),
        "conv_ln_g": 1.0 + n(13, (Dp, D_CONV), 0.02),
        "conv_ln_b": n(14, (Dp, D_CONV), 0.02),
        "w_conv_out": n(15, (Dp, D_CONV, D_MODEL), D_CONV ** -0.5),
        "gla_norm_g": 1.0 + n(16, (Dp, HEAD_V), 0.02),
        "w_o": n(17, (Dp, D_MODEL, D_MODEL), BETA * D_MODEL ** -0.5),
        "ln1_g": 1.0 + n(18, (Dp, D_MODEL), 0.02),
        "ln1_b": n(19, (Dp, D_MODEL), 0.02),
        "w_rg": n(20, (Dp, D_MODEL, N_GROUPS), D_MODEL ** -0.5),
        "b_rg": n(21, (Dp, N_GROUPS), 0.01),
        "w_re": n(22, (Dp, D_MODEL, N_EXPERTS), D_MODEL ** -0.5),
        "b_re": n(23, (Dp, N_EXPERTS), 0.01),
        "w_e_gate": n(24, (Dp, N_EXPERTS, D_MODEL, D_EXPERT), D_MODEL ** -0.5),
        "w_e_up": n(25, (Dp, N_EXPERTS, D_MODEL, D_EXPERT), D_MODEL ** -0.5),
        "w_e_down": n(26, (Dp, N_EXPERTS, D_EXPERT, D_MODEL), BETA * D_EXPERT ** -0.5),
        "ln2_g": 1.0 + n(27, (Dp, D_MODEL), 0.02),
        "ln2_b": n(28, (Dp, D_MODEL), 0.02),
        "w_ple": n(29, (Dp, PLE_DIM, D_MODEL), PLE_DIM ** -0.5),
        "w_ple_gate": n(30, (Dp, D_MODEL, D_MODEL), D_MODEL ** -0.5),
    }


def reference(x_prompt, x_sample, p_prompt, p_sample, state_conv, state_gla, ln_in_g, ln_in_b, w_in,
              w_gate_up, b_gate, conv_w, conv_b, conv_ln_g, conv_ln_b, w_conv_out, gla_norm_g, w_o, ln1_g,
              ln1_b, w_rg, b_rg, w_re, b_re, w_e_gate, w_e_up, w_e_down, ln2_g, ln2_b, w_ple, w_ple_gate):
    hp = layer_norm(x_prompt, ln_in_g, ln_in_b)
    hs = layer_norm(x_sample, ln_in_g, ln_in_b)
    B = x_prompt.shape[0]
    conv_p, gla_p, conv_s, gla_s = [], [], [], []
    for i in range(DEPTH):
        lw = [w[i] for w in (w_in, w_gate_up, b_gate, conv_w, conv_b, conv_ln_g, conv_ln_b, w_conv_out,
                             gla_norm_g, w_o, ln1_g, ln1_b, w_rg, b_rg, w_re, b_re, w_e_gate, w_e_up,
                             w_e_down, ln2_g, ln2_b, w_ple, w_ple_gate)]
        zc = jnp.zeros((B, CONV_STATE, D_CONV), hp.dtype)
        zs = jnp.zeros((B, N_HEADS, HEAD_K, HEAD_V), jnp.float32)
        hp, cp, sp = encoder_layer(hp, p_prompt[i], zc, zs, *lw)
        hs, cs, ss = encoder_layer(hs, p_sample[i], state_conv[i], state_gla[i], *lw)
        conv_p.append(cp); gla_p.append(sp); conv_s.append(cs); gla_s.append(ss)
    return (hp, hs, jnp.stack(conv_p), jnp.stack(gla_p), jnp.stack(conv_s), jnp.stack(gla_s))
```

```python
import functools

import numpy as np
import jax
import jax.numpy as jnp
from jax import lax
from jax.experimental import pallas as pl
from jax.experimental.pallas import tpu as pltpu

D_MODEL = 1024
D_CONV = 1024
CONV_WIDTH = 31
CONV_STATE = CONV_WIDTH - 1
HIST = 32
N_HEADS = 4
KEY_DIM = 512
VAL_DIM = 1024
HEAD_K = 128
HEAD_V = 256
GATE_RANK = 16
GATE_NORM = 16.0
CHUNK = 64
N_GROUPS = 4
EXPERTS_PER_GROUP = 8
N_EXPERTS = 32
D_EXPERT = 512
PLE_DIM = 256
LN_EPS = 1e-5
RMS_EPS = 1e-6
ALPHA = 2.0 ** 0.25
LANES = 128
SUBLANES = 8
N_PAIRS = EXPERTS_PER_GROUP * (EXPERTS_PER_GROUP - 1) // 2
N_BUCKETS = N_GROUPS * N_PAIRS
N_MAIN = 2 * D_CONV + 2 * KEY_DIM + 2 * VAL_DIM + 2 * D_MODEL

O_CVA, O_CVB, O_Q, O_K, O_V, O_G, O_GA, O_GB = 0, 1024, 2048, 2560, 3072, 4096, 5120, 6144

F32 = jnp.float32
BF16 = jnp.bfloat16


def _ln(x, g, b):
    mu = jnp.mean(x, axis=-1, keepdims=True)
    xc = x - mu
    var = jnp.mean(xc * xc, axis=-1, keepdims=True)
    return xc * lax.rsqrt(var + LN_EPS) * g + b


def _sigmoid(x):
    return 1.0 / (1.0 + jnp.exp(-x))


def _silu(x):
    return x * _sigmoid(x)


def _log_sigmoid(x):
    return jnp.minimum(x, 0.0) - jnp.log1p(jnp.exp(-jnp.abs(x)))


def _dot(a, b):
    return jnp.dot(a, b, preferred_element_type=F32)


def _dot_nt(a, b):
    return lax.dot_general(a, b, (((1,), (1,)), ((), ())), preferred_element_type=F32)


def _store_rows_linear(ref, val, n_rows):
    for ct in range(D_MODEL // LANES):
        ref[pl.ds(ct, n_rows, stride=SUBLANES), :] = val[:, ct * LANES:(ct + 1) * LANES]


def _load_rows_linear(ref, n_rows):
    return jnp.concatenate(
        [ref[pl.ds(ct, n_rows, stride=SUBLANES), :] for ct in range(D_MODEL // LANES)], axis=-1)


def _mixer_kernel(x_ref, cst_ref, gst_ref, lng_ref, lnb_ref, w_in_ref, wz_ref, wgu_ref, bg_ref,
                  convw_ref, convb_ref, clng_ref, clnb_ref, wco_ref, gng_ref, wo_ref, ln1g_ref, ln1b_ref,
                  wr_ref, br_ref, *rest, tl, chunk, aliased):
    if aliased:
        rest = rest[2:]
    h1_ref, meta_ref, cst_out_ref, gst_out_ref, ubuf, s_ref, o_ref = rest
    l = pl.program_id(1)

    @pl.when(l == 0)
    def _():
        ubuf[0:HIST, :] = cst_ref[0]
        s_ref[...] = gst_ref[0]

    x = x_ref[0]
    h = _ln(x, lng_ref[...], lnb_ref[...])
    hb = h.astype(BF16)

    def proj(lo, width):
        return _dot(hb, w_in_ref[:, lo:lo + width])

    u = proj(O_CVA, D_CONV) * _sigmoid(proj(O_CVB, D_CONV))
    ubuf[HIST:HIST + tl, :] = u
    acc = jnp.broadcast_to(convb_ref[...], (tl, D_CONV))
    for j in range(CONV_WIDTH):
        off = HIST - CONV_STATE + j
        acc = acc + convw_ref[j:j + 1, :] * ubuf[off:off + tl, :]
    a = _silu(_ln(acc, clng_ref[...], clnb_ref[...]))
    merged = _sigmoid(proj(O_GA, D_MODEL)) * _dot(a.astype(BF16), wco_ref[...])
    tail = ubuf[tl:tl + HIST, :]
    cst_out_ref[0] = tail
    ubuf[0:HIST, :] = tail

    q = proj(O_Q, KEY_DIM) * (HEAD_K ** -0.5)
    k = proj(O_K, KEY_DIM)
    v = proj(O_V, VAL_DIM)
    z = _dot(hb, wz_ref[...])
    la = _log_sigmoid(_dot(z.astype(BF16), wgu_ref[...]) + bg_ref[...]) * (1.0 / GATE_NORM)
    row = lax.broadcasted_iota(jnp.int32, (chunk, chunk), 0)
    col = lax.broadcasted_iota(jnp.int32, (chunk, chunk), 1)
    causal = row >= col
    tri = causal.astype(F32)
    for c in range(tl // chunk):
        r0 = c * chunk
        bc = jnp.dot(tri, la[r0:r0 + chunk, :], preferred_element_type=F32,
                     precision=lax.Precision.HIGHEST)
        b_mid = bc[chunk // 2:chunk // 2 + 1, :]
        b_last = bc[chunk - 1:chunk, :]
        qc = q[r0:r0 + chunk, :]
        kc = k[r0:r0 + chunk, :]
        qe = (qc * jnp.exp(bc - b_mid)).astype(BF16)
        ke = (kc * jnp.exp(b_mid - bc)).astype(BF16)
        q_in = (qc * jnp.exp(bc)).astype(BF16)
        k_out = (kc * jnp.exp(b_last - bc)).astype(BF16)
        decay = jnp.exp(b_last)
        for hh in range(N_HEADS):
            k0, v0 = hh * HEAD_K, hh * HEAD_V
            vh = v[r0:r0 + chunk, v0:v0 + HEAD_V]
            scores = _dot_nt(qe[:, k0:k0 + HEAD_K], ke[:, k0:k0 + HEAD_K])
            scores = jnp.where(causal, scores, 0.0)
            st = s_ref[hh]
            o = _dot(scores.astype(BF16), vh.astype(BF16)) + _dot_nt(q_in[:, k0:k0 + HEAD_K], st.astype(BF16))
            s_ref[hh] = st * decay[:, k0:k0 + HEAD_K] + _dot(vh.T.astype(BF16), k_out[:, k0:k0 + HEAD_K])
            o = o * lax.rsqrt(jnp.mean(o * o, axis=-1, keepdims=True) + RMS_EPS) * gng_ref[...]
            o_ref[r0:r0 + chunk, v0:v0 + HEAD_V] = o
    gst_out_ref[0] = s_ref[...]
    merged = merged + _sigmoid(proj(O_GB, D_MODEL)) * (o_ref[...] * _silu(proj(O_G, VAL_DIM)))

    h1 = _ln(ALPHA * h + _dot(merged.astype(BF16), wo_ref[...]), ln1g_ref[...], ln1b_ref[...])
    _store_rows_linear(h1_ref, h1, tl)

    logits = jnp.dot(h1, wr_ref[...], preferred_element_type=F32,
                     precision=lax.Precision.HIGHEST) + br_ref[...]
    lane = lax.broadcasted_iota(jnp.int32, (tl, LANES), 1)
    neg = -jnp.inf
    gl = jnp.where(lane < N_GROUPS, logits, neg)
    gmax = jnp.max(gl, axis=-1, keepdims=True)
    grp = jnp.min(jnp.where(gl == gmax, lane, LANES), axis=-1, keepdims=True)
    g_w = 1.0 / jnp.sum(jnp.exp(gl - gmax), axis=-1, keepdims=True)
    e0 = N_GROUPS + grp * EXPERTS_PER_GROUP
    el = jnp.where((lane >= e0) & (lane < e0 + EXPERTS_PER_GROUP), logits, neg)
    v1 = jnp.max(el, axis=-1, keepdims=True)
    i1 = jnp.min(jnp.where(el == v1, lane, LANES), axis=-1, keepdims=True)
    el2 = jnp.where(lane == i1, neg, el)
    v2 = jnp.max(el2, axis=-1, keepdims=True)
    i2 = jnp.min(jnp.where(el2 == v2, lane, LANES), axis=-1, keepdims=True)
    e21 = jnp.exp(v2 - v1)
    wt1 = g_w * (1.0 / (1.0 + e21))
    wt2 = g_w * (e21 / (1.0 + e21))
    first_lo = i1 < i2
    e_lo = (jnp.minimum(i1, i2) - N_GROUPS).astype(F32)
    e_hi = (jnp.maximum(i1, i2) - N_GROUPS).astype(F32)
    w_lo = jnp.where(first_lo, wt1, wt2)
    w_hi = jnp.where(first_lo, wt2, wt1)
    meta_ref[...] = jnp.where(lane == 0, e_lo, jnp.where(lane == 1, e_hi,
                              jnp.where(lane == 2, w_lo, jnp.where(lane == 3, w_hi, 0.0))))


def _mixer_call(x, cst, gst, wts, *, tl, row_off, total_rows, prev=None):
    bsz, seq, _ = x.shape
    assert seq % tl == 0 and row_off % tl == 0
    chunk = min(CHUNK, seq)
    assert tl % chunk == 0
    nl = seq // tl
    blk_off = row_off // tl
    const2 = lambda b, l: (0, 0)
    vec = lambda n: pl.BlockSpec((1, n), const2)
    mat = lambda r, c: pl.BlockSpec((r, c), const2)
    in_specs = [
        pl.BlockSpec((1, tl, D_MODEL), lambda b, l: (b, l, 0)),
        pl.BlockSpec((1, HIST, D_CONV), lambda b, l: (b, 0, 0)),
        pl.BlockSpec((1, N_HEADS, HEAD_V, HEAD_K), lambda b, l: (b, 0, 0, 0)),
        vec(D_MODEL), vec(D_MODEL),
        mat(D_MODEL, N_MAIN), mat(D_MODEL, LANES), mat(LANES, KEY_DIM), vec(KEY_DIM),
        mat(HIST, D_CONV), vec(D_CONV), vec(D_CONV), vec(D_CONV),
        mat(D_CONV, D_MODEL), vec(HEAD_V), mat(D_MODEL, D_MODEL), vec(D_MODEL), vec(D_MODEL),
        mat(D_MODEL, LANES), vec(LANES),
    ]
    args = [x, cst, gst] + list(wts)
    aliases = {}
    if prev is not None:
        in_specs += [pl.BlockSpec(memory_space=pl.ANY), pl.BlockSpec(memory_space=pl.ANY)]
        aliases = {len(args): 0, len(args) + 1: 1}
        args += list(prev)
    out_shape = (
        jax.ShapeDtypeStruct((total_rows * SUBLANES, LANES), F32),
        jax.ShapeDtypeStruct((total_rows, LANES), F32),
        jax.ShapeDtypeStruct((bsz, HIST, D_CONV), F32),
        jax.ShapeDtypeStruct((bsz, N_HEADS, HEAD_V, HEAD_K), F32),
    )
    out_specs = (
        pl.BlockSpec((tl * SUBLANES, LANES), lambda b, l: (blk_off + b * nl + l, 0)),
        pl.BlockSpec((tl, LANES), lambda b, l: (blk_off + b * nl + l, 0)),
        pl.BlockSpec((1, HIST, D_CONV), lambda b, l: (b, 0, 0)),
        pl.BlockSpec((1, N_HEADS, HEAD_V, HEAD_K), lambda b, l: (b, 0, 0, 0)),
    )
    return pl.pallas_call(
        functools.partial(_mixer_kernel, tl=tl, chunk=chunk, aliased=prev is not None),
        grid=(bsz, nl),
        in_specs=in_specs,
        out_specs=out_specs,
        out_shape=out_shape,
        scratch_shapes=[
            pltpu.VMEM((HIST + tl, D_CONV), F32),
            pltpu.VMEM((N_HEADS, HEAD_V, HEAD_K), F32),
            pltpu.VMEM((tl, VAL_DIM), F32),
        ],
        input_output_aliases=aliases,
        compiler_params=pltpu.CompilerParams(
            dimension_semantics=("arbitrary", "arbitrary"),
            vmem_limit_bytes=_VMEM_LIMIT),
        name="mixer",
    )(*args)


def _expert_kernel(te1_ref, te2_ref, nt_ref, ord_ref, h1_hbm, wlo_ref, whi_ref,
                   wg1_ref, wu1_ref, wd1_ref, wg2_ref, wu2_ref, wd2_ref, ln2g_ref, ln2b_ref,
                   y_hbm, xbuf, ybuf, gsem, ssem, *, tm, n_rows, n_tiles_max):
    i = pl.program_id(0)
    nt = nt_ref[0]
    slot = i % 2

    def row_copy_in(tile, j, sl):
        t = jnp.minimum(ord_ref[tile * tm + j], n_rows - 1)
        return pltpu.make_async_copy(
            h1_hbm.at[pl.ds(pl.multiple_of(t * SUBLANES, SUBLANES), SUBLANES), :],
            xbuf.at[sl, pl.ds(pl.multiple_of(j * SUBLANES, SUBLANES), SUBLANES), :],
            gsem.at[sl])

    def row_copy_out(tile, j, sl):
        t = ord_ref[tile * tm + j]
        return pltpu.make_async_copy(
            ybuf.at[sl, pl.ds(pl.multiple_of(j * SUBLANES, SUBLANES), SUBLANES), :],
            y_hbm.at[pl.ds(pl.multiple_of(t * SUBLANES, SUBLANES), SUBLANES), :],
            ssem.at[sl])

    def start_gather(tile, sl):
        def body(j, carry):
            row_copy_in(tile, j, sl).start()
            return carry
        lax.fori_loop(0, tm, body, 0, unroll=8)

    def wait_rows(copy_fn, tile, sl):
        def body(j, carry):
            copy_fn(tile, j, sl).wait()
            return carry
        lax.fori_loop(0, tm, body, 0, unroll=8)

    @pl.when(jnp.logical_and(i == 0, nt > 0))
    def _():
        start_gather(0, 0)

    @pl.when(i + 1 < nt)
    def _():
        start_gather(i + 1, 1 - slot)

    @pl.when(jnp.logical_and(i >= 2, i - 2 < nt))
    def _():
        wait_rows(row_copy_out, i - 2, slot)

    @pl.when(i < nt)
    def _():
        wait_rows(row_copy_in, i, slot)
        x = _load_rows_linear(xbuf.at[slot], tm)
        xb = x.astype(BF16)

        def expert(wg_ref, wu_ref, wd_ref):
            hg = _dot(xb, wg_ref[0])
            hu = _dot(xb, wu_ref[0])
            return _dot((_silu(hg) * hu).astype(BF16), wd_ref[0])

        moe = wlo_ref[...] * expert(wg1_ref, wu1_ref, wd1_ref) + whi_ref[...] * expert(wg2_ref, wu2_ref, wd2_ref)
        y = _ln(ALPHA * x + moe, ln2g_ref[...], ln2b_ref[...])
        _store_rows_linear(ybuf.at[slot], y, tm)

        def body(j, carry):
            row_copy_out(i, j, slot).start()
            return carry
        lax.fori_loop(0, tm, body, 0, unroll=8)

    @pl.when(i == n_tiles_max - 1)
    def _():
        @pl.when(jnp.logical_and(i >= 1, i - 1 < nt))
        def _():
            wait_rows(row_copy_out, i - 1, 1 - slot)

        @pl.when(i < nt)
        def _():
            wait_rows(row_copy_out, i, slot)


def _expert_call(te1, te2, nt, order, h1, wlo, whi, wg, wu, wd, ln2g, ln2b, *, tm, n_rows, n_tiles_max):
    wspec = lambda which, r, c: pl.BlockSpec(
        (1, r, c), (lambda i, te1, te2, nt, o: (te1[i], 0, 0)) if which == 1 else
        (lambda i, te1, te2, nt, o: (te2[i], 0, 0)))
    rowspec = pl.BlockSpec((tm, 1), lambda i, te1, te2, nt, o: (i, 0))
    vec = pl.BlockSpec((1, D_MODEL), lambda i, te1, te2, nt, o: (0, 0))
    grid_spec = pltpu.PrefetchScalarGridSpec(
        num_scalar_prefetch=4,
        grid=(n_tiles_max,),
        in_specs=[
            pl.BlockSpec(memory_space=pl.ANY), rowspec, rowspec,
            wspec(1, D_MODEL, D_EXPERT), wspec(1, D_MODEL, D_EXPERT), wspec(1, D_EXPERT, D_MODEL),
            wspec(2, D_MODEL, D_EXPERT), wspec(2, D_MODEL, D_EXPERT), wspec(2, D_EXPERT, D_MODEL),
            vec, vec,
        ],
        out_specs=pl.BlockSpec(memory_space=pl.ANY),
        scratch_shapes=[
            pltpu.VMEM((2, tm * SUBLANES, LANES), F32),
            pltpu.VMEM((2, tm * SUBLANES, LANES), F32),
            pltpu.SemaphoreType.DMA((2,)),
            pltpu.SemaphoreType.DMA((2,)),
        ],
    )
    return pl.pallas_call(
        functools.partial(_expert_kernel, tm=tm, n_rows=n_rows, n_tiles_max=n_tiles_max),
        grid_spec=grid_spec,
        out_shape=jax.ShapeDtypeStruct(((n_rows + 2 * tm) * SUBLANES, LANES), F32),
        compiler_params=pltpu.CompilerParams(
            dimension_semantics=("arbitrary",),
            vmem_limit_bytes=_VMEM_LIMIT),
        name="experts",
    )(te1, te2, nt, order, h1, wlo, whi, wg, wu, wd, wg, wu, wd, ln2g, ln2b)


def _ple_kernel(y_ref, p_ref, wple_ref, wpg_ref, out_ref, *, tl):
    y = _load_rows_linear(y_ref, tl)
    gate = _sigmoid(_dot(y.astype(BF16), wpg_ref[...]))
    out_ref[...] = y + gate * _dot(p_ref[...].astype(BF16), wple_ref[...])


def _ple_call(y2, p, wple, wpg, *, tl, row_off, n_tok):
    assert n_tok % tl == 0 and row_off % tl == 0
    blk_off = row_off // tl
    const2 = lambda i: (0, 0)
    return pl.pallas_call(
        functools.partial(_ple_kernel, tl=tl),
        grid=(n_tok // tl,),
        in_specs=[
            pl.BlockSpec((tl * SUBLANES, LANES), lambda i: (blk_off + i, 0)),
            pl.BlockSpec((tl, PLE_DIM), lambda i: (i, 0)),
            pl.BlockSpec((PLE_DIM, D_MODEL), const2),
            pl.BlockSpec((D_MODEL, D_MODEL), const2),
        ],
        out_specs=pl.BlockSpec((tl, D_MODEL), lambda i: (i, 0)),
        out_shape=jax.ShapeDtypeStruct((n_tok, D_MODEL), F32),
        compiler_params=pltpu.CompilerParams(
            dimension_semantics=("arbitrary",),
            vmem_limit_bytes=_VMEM_LIMIT),
        name="ple",
    )(y2, p, wple, wpg)


_VMEM_LIMIT = 56 * 1024 * 1024


def _pair_tables():
    lo, hi = np.triu_indices(EXPERTS_PER_GROUP, k=1)
    e1 = (np.arange(N_GROUPS)[:, None] * EXPERTS_PER_GROUP + lo[None, :]).reshape(-1)
    e2 = (np.arange(N_GROUPS)[:, None] * EXPERTS_PER_GROUP + hi[None, :]).reshape(-1)
    return e1.astype(np.int32), e2.astype(np.int32)


def _dispatch_tables(meta, n_rows, tm, n_tiles_max):
    e_lo = meta[:, 0].astype(jnp.int32)
    e_hi = meta[:, 1].astype(jnp.int32)
    grp = e_lo // EXPERTS_PER_GROUP
    lo = e_lo % EXPERTS_PER_GROUP
    hi = e_hi % EXPERTS_PER_GROUP
    bucket = grp * N_PAIRS + (lo * (2 * EXPERTS_PER_GROUP - 1 - lo)) // 2 + (hi - lo - 1)
    counts = jnp.zeros((N_BUCKETS,), jnp.int32).at[bucket].add(1)
    padded = ((counts + tm - 1) // tm) * tm
    pad_end = jnp.cumsum(padded)
    pad_start = pad_end - padded
    raw_start = jnp.cumsum(counts) - counts
    order = jnp.argsort(bucket, stable=True).astype(jnp.int32)
    sb = bucket[order]
    dest = pad_start[sb] + (jnp.arange(n_rows, dtype=jnp.int32) - raw_start[sb])
    slots = jnp.arange(n_tiles_max * tm, dtype=jnp.int32)
    dump = n_rows + ((slots // tm) % 2) * tm + slots % tm
    ord_arr = dump.at[dest].set(order)
    zeros = jnp.zeros((n_tiles_max * tm,), F32)
    wlo = zeros.at[dest].set(meta[order, 2]).reshape(-1, 1)
    whi = zeros.at[dest].set(meta[order, 3]).reshape(-1, 1)
    nt = (pad_end[-1] // tm).astype(jnp.int32)
    tile_start = jnp.minimum(jnp.arange(n_tiles_max, dtype=jnp.int32), jnp.maximum(nt - 1, 0)) * tm
    tile_bucket = jnp.minimum(jnp.searchsorted(pad_end, tile_start, side="right"), N_BUCKETS - 1)
    pe1, pe2 = _pair_tables()
    te1 = jnp.asarray(pe1)[tile_bucket]
    te2 = jnp.asarray(pe2)[tile_bucket]
    return te1, te2, nt.reshape(1), ord_arr, wlo, whi


def _forward(x_prompt, x_sample, p_prompt, p_sample, state_conv, state_gla, ln_in_g, ln_in_b, w_in, w_gate_up,
             b_gate, conv_w, conv_b, conv_ln_g, conv_ln_b, w_conv_out, gla_norm_g, w_o, ln1_g, ln1_b, w_rg, b_rg,
             w_re, b_re, w_e_gate, w_e_up, w_e_down, ln2_g, ln2_b, w_ple, w_ple_gate, *, tl, tm, tl_ple):
    bp, lp, _ = x_prompt.shape
    bs, ls, _ = x_sample.shape
    n_p, n_s = bp * lp, bs * ls
    n_rows = n_p + n_s
    row = lambda a: a.reshape(1, -1).astype(F32)

    w_in0 = w_in[0]
    wz = jnp.pad(w_in0[:, N_MAIN:], ((0, 0), (0, LANES - GATE_RANK))).astype(BF16)
    wgu = jnp.pad(w_gate_up[0], ((0, LANES - GATE_RANK), (0, 0))).astype(BF16)
    wr = jnp.pad(jnp.concatenate([w_rg[0], w_re[0]], axis=1),
                 ((0, 0), (0, LANES - N_GROUPS - N_EXPERTS))).astype(F32)
    br = jnp.pad(jnp.concatenate([b_rg[0], b_re[0]]), (0, LANES - N_GROUPS - N_EXPERTS)).reshape(1, LANES)
    wts = [
        row(ln_in_g), row(ln_in_b),
        w_in0[:, :N_MAIN].astype(BF16), wz, wgu, row(b_gate[0]),
        jnp.pad(conv_w[0], ((0, HIST - CONV_WIDTH), (0, 0))), row(conv_b[0]), row(conv_ln_g[0]), row(conv_ln_b[0]),
        w_conv_out[0].astype(BF16), row(gla_norm_g[0]), w_o[0].astype(BF16), row(ln1_g[0]), row(ln1_b[0]),
        wr, br,
    ]

    zc = jnp.zeros((bp, HIST, D_CONV), F32)
    zs = jnp.zeros((bp, N_HEADS, HEAD_V, HEAD_K), F32)
    h1, meta, conv_p, gla_p = _mixer_call(x_prompt, zc, zs, wts, tl=tl, row_off=0, total_rows=n_rows)
    sc = jnp.pad(state_conv[0], ((0, 0), (HIST - CONV_STATE, 0), (0, 0)))
    sg = jnp.swapaxes(state_gla[0], -1, -2)
    h1, meta, conv_s, gla_s = _mixer_call(x_sample, sc, sg, wts, tl=ls, row_off=n_p, total_rows=n_rows,
                                          prev=(h1, meta))

    n_tiles_max = (n_rows + N_BUCKETS * (tm - 1)) // tm + 1
    te1, te2, nt, ord_arr, wlo, whi = _dispatch_tables(meta, n_rows, tm, n_tiles_max)
    y2 = _expert_call(te1, te2, nt, ord_arr, h1, wlo, whi,
                      w_e_gate[0].astype(BF16), w_e_up[0].astype(BF16), w_e_down[0].astype(BF16),
                      row(ln2_g[0]), row(ln2_b[0]), tm=tm, n_rows=n_rows, n_tiles_max=n_tiles_max)

    wple = w_ple[0].astype(BF16)
    wpg = w_ple_gate[0].astype(BF16)
    y_p = _ple_call(y2, p_prompt[0].reshape(n_p, PLE_DIM), wple, wpg, tl=tl_ple, row_off=0, n_tok=n_p)
    y_s = _ple_call(y2, p_sample[0].reshape(n_s, PLE_DIM), wple, wpg, tl=n_s, row_off=n_p, n_tok=n_s)

    fix_conv = lambda c: c[:, HIST - CONV_STATE:, :][None]
    fix_gla = lambda s: jnp.swapaxes(s, -1, -2)[None]
    return (y_p.reshape(bp, lp, D_MODEL), y_s.reshape(bs, ls, D_MODEL),
            fix_conv(conv_p), fix_gla(gla_p), fix_conv(conv_s), fix_gla(gla_s))


def kernel(x_prompt, x_sample, p_prompt, p_sample, state_conv, state_gla, ln_in_g, ln_in_b, w_in, w_gate_up, b_gate, conv_w, conv_b, conv_ln_g, conv_ln_b, w_conv_out, gla_norm_g, w_o, ln1_g, ln1_b, w_rg, b_rg, w_re, b_re, w_e_gate, w_e_up, w_e_down, ln2_g, ln2_b, w_ple, w_ple_gate):
    return _forward(x_prompt, x_sample, p_prompt, p_sample, state_conv, state_gla, ln_in_g, ln_in_b, w_in,
                    w_gate_up, b_gate, conv_w, conv_b, conv_ln_g, conv_ln_b, w_conv_out, gla_norm_g, w_o, ln1_g,
                    ln1_b, w_rg, b_rg, w_re, b_re, w_e_gate, w_e_up, w_e_down, ln2_g, ln2_b, w_ple, w_ple_gate,
                    tl=256, tm=256, tl_ple=512)
```

```python
import functools

import numpy as np
import jax
import jax.numpy as jnp
from jax import lax
from jax.experimental import pallas as pl
from jax.experimental.pallas import tpu as pltpu

D_MODEL = 1024
D_CONV = 1024
CONV_WIDTH = 31
CONV_STATE = CONV_WIDTH - 1
N_HEADS = 4
KEY_DIM = 512
VAL_DIM = 1024
HEAD_K = 128
HEAD_V = 256
GATE_RANK = 16
GATE_NORM = 16.0
CHUNK = 64
N_GROUPS = 4
EXPERTS_PER_GROUP = 8
N_EXPERTS = 32
D_EXPERT = 512
PLE_DIM = 256
LN_EPS = 1e-5
RMS_EPS = 1e-6
ALPHA = 2.0 ** 0.25
LANES = 128
SUBLANES = 8
HIST = 4 * SUBLANES
CONV_RB = 2 * SUBLANES
N_PAIRS = EXPERTS_PER_GROUP * (EXPERTS_PER_GROUP - 1) // 2
N_BUCKETS = N_GROUPS * N_PAIRS
N_MAIN = 2 * D_CONV + 2 * KEY_DIM + 2 * VAL_DIM + 2 * D_MODEL
HALF = D_MODEL // 2
W_SUB = HALF // LANES
_VMEM_LIMIT = 60 * 1024 * 1024

O_CVA, O_CVB, O_Q, O_K, O_V, O_G, O_GA, O_GB = 0, 1024, 2048, 2560, 3072, 4096, 5120, 6144

F32 = jnp.float32
BF16 = jnp.bfloat16
I32 = jnp.int32


def _ln(x, g, b):
    mu = jnp.mean(x, axis=-1, keepdims=True)
    xc = x - mu
    var = jnp.mean(xc * xc, axis=-1, keepdims=True)
    return xc * lax.rsqrt(var + LN_EPS) * g + b


def _sigmoid(x):
    return 1.0 / (1.0 + jnp.exp(-x))


def _silu(x):
    return x * _sigmoid(x)


def _log_sigmoid(x):
    return jnp.minimum(x, 0.0) - jnp.log1p(jnp.exp(-jnp.abs(x)))


def _dot(a, b):
    return jnp.dot(a, b, preferred_element_type=F32)


def _dot_nt(a, b):
    return lax.dot_general(a, b, (((1,), (1,)), ((), ())), preferred_element_type=F32)


def _store_rows_linear(ref, val, n_rows):
    for ct in range(val.shape[1] // LANES):
        ref[pl.ds(ct, n_rows, stride=SUBLANES), :] = val[:, ct * LANES:(ct + 1) * LANES]


def _load_rows_linear(ref, n_rows, n_sub):
    return jnp.concatenate(
        [ref[pl.ds(ct, n_rows, stride=SUBLANES), :] for ct in range(n_sub)], axis=-1)


def _log2(n):
    assert n & (n - 1) == 0
    return n.bit_length() - 1


def _route(h1, wr_ref, br_ref):
    tl = h1.shape[0]
    logits = jnp.dot(h1, wr_ref[...], preferred_element_type=F32,
                     precision=lax.Precision.HIGHEST) + br_ref[...]
    lane = lax.broadcasted_iota(I32, (tl, LANES), 1)
    neg = -jnp.inf
    gl = jnp.where(lane < N_GROUPS, logits, neg)
    gmax = jnp.max(gl, axis=-1, keepdims=True)
    grp = jnp.min(jnp.where(gl == gmax, lane, LANES), axis=-1, keepdims=True)
    g_w = 1.0 / jnp.sum(jnp.exp(gl - gmax), axis=-1, keepdims=True)
    e0 = N_GROUPS + grp * EXPERTS_PER_GROUP
    el = jnp.where((lane >= e0) & (lane < e0 + EXPERTS_PER_GROUP), logits, neg)
    v1 = jnp.max(el, axis=-1, keepdims=True)
    i1 = jnp.min(jnp.where(el == v1, lane, LANES), axis=-1, keepdims=True)
    el2 = jnp.where(lane == i1, neg, el)
    v2 = jnp.max(el2, axis=-1, keepdims=True)
    i2 = jnp.min(jnp.where(el2 == v2, lane, LANES), axis=-1, keepdims=True)
    e21 = jnp.exp(v2 - v1)
    wt1 = g_w * (1.0 / (1.0 + e21))
    wt2 = g_w * (e21 / (1.0 + e21))
    first_lo = i1 < i2
    lo = jnp.minimum(i1, i2) - e0
    hi = jnp.maximum(i1, i2) - e0
    bucket = grp * N_PAIRS + ((lo * (2 * EXPERTS_PER_GROUP - 1 - lo)) >> 1) + (hi - lo - 1)
    return bucket, jnp.where(first_lo, wt1, wt2), jnp.where(first_lo, wt2, wt1)


def _assign_slots(bucket, st_ref, tb_ref, *, tm, npl):
    tl = bucket.shape[0]
    sh = _log2(tm)
    bt = jnp.broadcast_to(bucket.astype(F32), (tl, LANES)).T
    sub = lax.broadcasted_iota(I32, (LANES, tl), 0)
    oh = sub == bt.astype(I32)
    ohf = oh.astype(F32)
    c_b = jnp.sum(ohf, axis=1, keepdims=True).astype(I32)
    before = lax.broadcasted_iota(I32, (tl, tl), 0) < lax.broadcasted_iota(I32, (tl, tl), 1)
    cum = _dot(ohf.astype(BF16), before.astype(BF16))
    rank = jnp.sum(ohf * cum, axis=0, keepdims=True).astype(I32)

    n_b = st_ref[:, 0:1]
    cur_b = st_ref[:, 1:2]
    nxt = st_ref[:, 2:3]
    part_b = (n_b & (tm - 1)) > 0
    full_b = n_b >> sh
    ceil_b = full_b + part_b.astype(I32)
    pnew_b = ((n_b + c_b + (tm - 1)) >> sh) - ceil_b
    below = lax.broadcasted_iota(I32, (LANES, LANES), 1) < lax.broadcasted_iota(I32, (LANES, LANES), 0)
    pn_f = jnp.broadcast_to(pnew_b.astype(F32), (LANES, LANES)).astype(BF16)
    base_b = nxt + _dot(below.astype(BF16), pn_f)[:, 0:1].astype(I32)
    total_new = jnp.sum(pnew_b.astype(F32), axis=0, keepdims=True).astype(I32)

    def per_token(col):
        return jnp.sum(jnp.where(oh, col.astype(F32), 0.0), axis=0, keepdims=True).astype(I32)

    n_t = per_token(n_b)
    g = n_t + rank
    q = g >> sh
    in_part = (q == (n_t >> sh)) & ((n_t & (tm - 1)) > 0)
    ceil_t = (n_t >> sh) + ((n_t & (tm - 1)) > 0).astype(I32)
    page = jnp.where(in_part, per_token(cur_b), per_token(base_b) + q - ceil_t)
    dest = page * tm + (g & (tm - 1))

    pg = lax.broadcasted_iota(I32, (LANES, npl), 1)
    owner = jnp.where((pg >= base_b) & (pg < base_b + pnew_b), lax.broadcasted_iota(I32, (LANES, npl), 0), 0)
    tb_ref[...] = tb_ref[...] + jnp.broadcast_to(
        jnp.sum(owner.astype(F32), axis=0, keepdims=True).astype(I32), (SUBLANES, npl))
    lane = lax.broadcasted_iota(I32, (LANES, LANES), 1)
    st_ref[...] = jnp.where(lane == 0, n_b + c_b,
                            jnp.where(lane == 1, jnp.where(pnew_b > 0, base_b + pnew_b - 1, cur_b),
                                      jnp.where(lane == 2, nxt + total_new, 0)))
    return dest


def _mixer_kernel(x_ref, cst_ref, gst_ref, st_in_ref, tb_in_ref, lng_ref, lnb_ref, w_in_ref, wz_ref, wgu_ref,
                  bg_ref, convw_ref, convb_ref, clng_ref, clnb_ref, wco_ref, gng_ref, wo_ref, ln1g_ref, ln1b_ref,
                  wr_ref, br_ref, xs_in_hbm,
                  h1_ref, dest_ref, cst_out_ref, gst_out_ref, st_out_ref, tb_out_ref, xs_hbm,
                  ubuf, ph_ref, s_ref, o_ref, abuf, xrow, dvm, dsm, st_s, tb_s, ssem, dsem,
                  *, nb, tls, chunk, tm, npl):
    del xs_in_hbm
    tl = nb * tls
    step = pl.program_id(0) * pl.num_programs(1) + pl.program_id(1)
    n_steps = pl.num_programs(0) * pl.num_programs(1)
    slot = step % 2

    @pl.when(step == 0)
    def _():
        st_s[...] = st_in_ref[...]
        tb_s[...] = tb_in_ref[...]
        xrow[...] = jnp.zeros(xrow.shape, I32)

    @pl.when(pl.program_id(1) == 0)
    def _():
        ubuf[:, 0:HIST, :] = cst_ref[...]
        s_ref[...] = gst_ref[...]

    x = x_ref[...].reshape(tl, D_MODEL)
    h = _ln(x, lng_ref[...], lnb_ref[...])
    hb = h.astype(BF16)

    def proj(lo, width):
        return _dot(hb, w_in_ref[:, lo:lo + width])

    u = proj(O_CVA, D_CONV) * _sigmoid(proj(O_CVB, D_CONV))
    ubuf[:, HIST:HIST + tls, :] = u.reshape(nb, tls, D_CONV)
    n_ph = tls + HIST - SUBLANES
    for s in range(nb):
        for p in range(1, SUBLANES):
            ph_ref[p - 1] = ubuf[s, p:p + n_ph, :]

        def conv_rows(rb, carry, s=s):
            r0 = pl.multiple_of(rb * CONV_RB, CONV_RB)
            acc = jnp.broadcast_to(convb_ref[...], (CONV_RB, D_CONV))
            for j in range(CONV_WIDTH):
                shift = HIST - CONV_STATE + j
                p, a = shift % SUBLANES, shift - shift % SUBLANES
                rows = pl.ds(pl.multiple_of(r0 + a, SUBLANES), CONV_RB)
                src = ubuf[s, rows, :] if p == 0 else ph_ref[p - 1, rows, :]
                acc = acc + convw_ref[j * CONV_RB:(j + 1) * CONV_RB, :] * src
            act = _silu(_ln(acc, clng_ref[...], clnb_ref[...]))
            abuf[pl.ds(pl.multiple_of(s * tls + r0, CONV_RB), CONV_RB), :] = act.astype(BF16)
            return carry

        lax.fori_loop(0, tls // CONV_RB, conv_rows, 0)
    tail = ubuf[:, tls:tls + HIST, :]
    cst_out_ref[...] = tail
    ubuf[:, 0:HIST, :] = tail
    merged = _sigmoid(proj(O_GA, D_MODEL)) * _dot(abuf[...], wco_ref[...])

    q = proj(O_Q, KEY_DIM) * (HEAD_K ** -0.5)
    k = proj(O_K, KEY_DIM)
    v = proj(O_V, VAL_DIM)
    z = _dot(hb, wz_ref[...])
    la = _log_sigmoid(_dot(z.astype(BF16), wgu_ref[...]) + bg_ref[...]) * (1.0 / GATE_NORM)
    row = lax.broadcasted_iota(I32, (chunk, chunk), 0)
    col = lax.broadcasted_iota(I32, (chunk, chunk), 1)
    causal = row >= col
    tri = causal.astype(F32)
    for s in range(nb):
        for c in range(tls // chunk):
            r0 = s * tls + c * chunk
            bc = jnp.dot(tri, la[r0:r0 + chunk, :], preferred_element_type=F32,
                         precision=lax.Precision.HIGHEST)
            b_mid = bc[chunk // 2:chunk // 2 + 1, :]
            b_last = bc[chunk - 1:chunk, :]
            qc = q[r0:r0 + chunk, :]
            kc = k[r0:r0 + chunk, :]
            qe = (qc * jnp.exp(bc - b_mid)).astype(BF16)
            ke = (kc * jnp.exp(b_mid - bc)).astype(BF16)
            q_in = (qc * jnp.exp(bc)).astype(BF16)
            k_out = (kc * jnp.exp(b_last - bc)).astype(BF16)
            decay = jnp.exp(b_last)
            for hh in range(N_HEADS):
                k0, v0 = hh * HEAD_K, hh * HEAD_V
                vh = v[r0:r0 + chunk, v0:v0 + HEAD_V]
                scores = _dot_nt(qe[:, k0:k0 + HEAD_K], ke[:, k0:k0 + HEAD_K])
                scores = jnp.where(causal, scores, 0.0)
                st = s_ref[s, hh]
                o = (_dot(scores.astype(BF16), vh.astype(BF16))
                     + _dot_nt(q_in[:, k0:k0 + HEAD_K], st.astype(BF16)))
                s_ref[s, hh] = (st * decay[:, k0:k0 + HEAD_K]
                                + _dot(vh.T.astype(BF16), k_out[:, k0:k0 + HEAD_K]))
                o = o * lax.rsqrt(jnp.mean(o * o, axis=-1, keepdims=True) + RMS_EPS) * gng_ref[...]
                o_ref[r0:r0 + chunk, v0:v0 + HEAD_V] = o
    gst_out_ref[...] = s_ref[...]
    merged = merged + _sigmoid(proj(O_GB, D_MODEL)) * (o_ref[...] * _silu(proj(O_G, VAL_DIM)))

    h1 = _ln(ALPHA * h + _dot(merged.astype(BF16), wo_ref[...]), ln1g_ref[...], ln1b_ref[...])
    h1_ref[...] = h1

    bucket, w_lo, w_hi = _route(h1, wr_ref, br_ref)
    dest = _assign_slots(bucket, st_s, tb_s, tm=tm, npl=npl)
    st_out_ref[...] = st_s[...]
    tb_out_ref[...] = tb_s[...]
    dest8 = jnp.broadcast_to(dest, (SUBLANES, tl))
    dest_ref[0] = dest8
    dvm[...] = dest8
    to_smem = pltpu.make_async_copy(dvm, dsm, dsem.at[0])
    to_smem.start()

    def scatter_done(sl):
        pltpu.make_async_copy(xrow.at[sl], xs_hbm.at[pl.ds(0, tl * SUBLANES), :], ssem.at[sl]).wait()

    @pl.when(step >= 2)
    def _():
        scatter_done(slot)

    hbits = lax.bitcast_convert_type(h1.astype(BF16).astype(F32), I32)
    packed = hbits[:, :HALF] | lax.shift_right_logical(hbits[:, HALF:], 16)
    _store_rows_linear(xrow.at[slot], packed, tl)
    lane = lax.broadcasted_iota(I32, (tl, LANES), 1)
    wrow = jnp.where(lane == 0, w_lo, jnp.where(lane == 1, w_hi, 0.0))
    xrow[slot, pl.ds(W_SUB, tl, stride=SUBLANES), :] = lax.bitcast_convert_type(wrow, I32)

    to_smem.wait()

    def issue(j, carry):
        d = dsm[0, j]
        pltpu.make_async_copy(
            xrow.at[slot, pl.ds(pl.multiple_of(j * SUBLANES, SUBLANES), SUBLANES), :],
            xs_hbm.at[pl.ds(pl.multiple_of(d * SUBLANES, SUBLANES), SUBLANES), :],
            ssem.at[slot]).start()
        return carry

    lax.fori_loop(0, tl, issue, 0, unroll=8)

    @pl.when(step == n_steps - 1)
    def _():
        @pl.when(step >= 1)
        def _():
            scatter_done(1 - slot)
        scatter_done(slot)


def _mixer_call(x, cst, gst, st, tb, wts, xs, *, nb, tls, tm, npl):
    bsz, seq, _ = x.shape
    assert bsz % nb == 0 and seq % tls == 0 and tls % CONV_RB == 0
    chunk = min(CHUNK, seq)
    assert tls % chunk == 0
    tl = nb * tls
    assert tl % LANES == 0
    nl = seq // tls
    n_steps = (bsz // nb) * nl
    const2 = lambda b, l: (0, 0)
    vec = lambda n: pl.BlockSpec((1, n), const2)
    mat = lambda r, c: pl.BlockSpec((r, c), const2, pipeline_mode=pl.Buffered(1))
    in_specs = [
        pl.BlockSpec((nb, tls, D_MODEL), lambda b, l: (b, l, 0)),
        pl.BlockSpec((nb, HIST, D_CONV), lambda b, l: (b, 0, 0)),
        pl.BlockSpec((nb, N_HEADS, HEAD_V, HEAD_K), lambda b, l: (b, 0, 0, 0)),
        mat(LANES, LANES), mat(SUBLANES, npl),
        vec(D_MODEL), vec(D_MODEL),
        mat(D_MODEL, N_MAIN), mat(D_MODEL, LANES), mat(LANES, KEY_DIM), vec(KEY_DIM),
        mat(CONV_WIDTH * CONV_RB, D_CONV), vec(D_CONV), vec(D_CONV), vec(D_CONV),
        mat(D_CONV, D_MODEL), vec(HEAD_V), mat(D_MODEL, D_MODEL), vec(D_MODEL), vec(D_MODEL),
        mat(D_MODEL, LANES), vec(LANES),
        pl.BlockSpec(memory_space=pl.ANY),
    ]
    args = [x, cst, gst, st, tb] + list(wts) + [xs]
    out_shape = (
        jax.ShapeDtypeStruct((bsz * seq, D_MODEL), F32),
        jax.ShapeDtypeStruct((n_steps, SUBLANES, tl), I32),
        jax.ShapeDtypeStruct((bsz, HIST, D_CONV), F32),
        jax.ShapeDtypeStruct((bsz, N_HEADS, HEAD_V, HEAD_K), F32),
        jax.ShapeDtypeStruct((LANES, LANES), I32),
        jax.ShapeDtypeStruct((SUBLANES, npl), I32),
        jax.ShapeDtypeStruct(xs.shape, xs.dtype),
    )
    out_specs = (
        pl.BlockSpec((tl, D_MODEL), lambda b, l: (b * nl + l, 0)),
        pl.BlockSpec((1, SUBLANES, tl), lambda b, l: (b * nl + l, 0, 0)),
        pl.BlockSpec((nb, HIST, D_CONV), lambda b, l: (b, 0, 0)),
        pl.BlockSpec((nb, N_HEADS, HEAD_V, HEAD_K), lambda b, l: (b, 0, 0, 0)),
        pl.BlockSpec((LANES, LANES), const2),
        pl.BlockSpec((SUBLANES, npl), const2),
        pl.BlockSpec(memory_space=pl.ANY),
    )
    return pl.pallas_call(
        functools.partial(_mixer_kernel, nb=nb, tls=tls, chunk=chunk, tm=tm, npl=npl),
        grid=(bsz // nb, nl),
        in_specs=in_specs,
        out_specs=out_specs,
        out_shape=out_shape,
        scratch_shapes=[
            pltpu.VMEM((nb, HIST + tls, D_CONV), F32),
            pltpu.VMEM((SUBLANES - 1, tls + HIST - SUBLANES, D_CONV), F32),
            pltpu.VMEM((nb, N_HEADS, HEAD_V, HEAD_K), F32),
            pltpu.VMEM((tl, VAL_DIM), F32),
            pltpu.VMEM((tl, D_CONV), BF16),
            pltpu.VMEM((2, tl * SUBLANES, LANES), I32),
            pltpu.VMEM((SUBLANES, tl), I32),
            pltpu.SMEM((SUBLANES, tl), I32),
            pltpu.VMEM((LANES, LANES), I32),
            pltpu.VMEM((SUBLANES, npl), I32),
            pltpu.SemaphoreType.DMA((2,)),
            pltpu.SemaphoreType.DMA((1,)),
        ],
        input_output_aliases={len(args) - 1: 6},
        compiler_params=pltpu.CompilerParams(
            dimension_semantics=("arbitrary", "arbitrary"),
            vmem_limit_bytes=_VMEM_LIMIT),
        name="mixer",
    )(*args)


def _expert_kernel(te1_ref, te2_ref, nt_ref, xs_ref, wg1_ref, wu1_ref, wd1_ref, wg2_ref, wu2_ref, wd2_ref,
                   ys_ref, *, tm):
    del te1_ref, te2_ref

    @pl.when(pl.program_id(0) < nt_ref[0])
    def _():
        words = _load_rows_linear(xs_ref, tm, W_SUB)
        hi = lax.bitcast_convert_type(words & jnp.int32(-65536), F32)
        lo = lax.bitcast_convert_type(lax.shift_left(words, 16), F32)
        xb = jnp.concatenate([hi, lo], axis=-1).astype(BF16)
        wrow = lax.bitcast_convert_type(xs_ref[pl.ds(W_SUB, tm, stride=SUBLANES), :], F32)

        def expert(wg_ref, wu_ref, wd_ref):
            hg = _dot(xb, wg_ref[0])
            hu = _dot(xb, wu_ref[0])
            return _dot((_silu(hg) * hu).astype(BF16), wd_ref[0])

        moe = wrow[:, 0:1] * expert(wg1_ref, wu1_ref, wd1_ref) + wrow[:, 1:2] * expert(wg2_ref, wu2_ref, wd2_ref)
        _store_rows_linear(ys_ref, moe, tm)


def _expert_call(te1, te2, nt, xs, wg, wu, wd, *, tm, n_pages):
    page = lambda i, te1, te2, nt: (jnp.minimum(i, jnp.maximum(nt[0] - 1, 0)), 0)
    wspec = lambda which, r, c: pl.BlockSpec(
        (1, r, c), (lambda i, te1, te2, nt: (te1[i], 0, 0)) if which == 1 else
        (lambda i, te1, te2, nt: (te2[i], 0, 0)))
    grid_spec = pltpu.PrefetchScalarGridSpec(
        num_scalar_prefetch=3,
        grid=(n_pages,),
        in_specs=[
            pl.BlockSpec((tm * SUBLANES, LANES), page),
            wspec(1, D_MODEL, D_EXPERT), wspec(1, D_MODEL, D_EXPERT), wspec(1, D_EXPERT, D_MODEL),
            wspec(2, D_MODEL, D_EXPERT), wspec(2, D_MODEL, D_EXPERT), wspec(2, D_EXPERT, D_MODEL),
        ],
        out_specs=pl.BlockSpec((tm * SUBLANES, LANES), page),
    )
    return pl.pallas_call(
        functools.partial(_expert_kernel, tm=tm),
        grid_spec=grid_spec,
        out_shape=jax.ShapeDtypeStruct((n_pages * tm * SUBLANES, LANES), F32),
        compiler_params=pltpu.CompilerParams(
            dimension_semantics=("arbitrary",),
            vmem_limit_bytes=_VMEM_LIMIT),
        name="experts",
    )(te1, te2, nt, xs, wg, wu, wd, wg, wu, wd)


def _combine_kernel(dest_ref, ys_hbm, h1_ref, p_ref, ln2g_ref, ln2b_ref, wple_ref, wpg_ref, out_ref,
                    ybuf, gsem, *, tl):
    i = pl.program_id(0)
    n = pl.num_programs(0)
    slot = i % 2

    def start_gather(tile, sl):
        def body(j, carry):
            t = tile * tl + j
            d = dest_ref[t // LANES, t % LANES]
            pltpu.make_async_copy(
                ys_hbm.at[pl.ds(pl.multiple_of(d * SUBLANES, SUBLANES), SUBLANES), :],
                ybuf.at[sl, pl.ds(pl.multiple_of(j * SUBLANES, SUBLANES), SUBLANES), :],
                gsem.at[sl]).start()
            return carry
        lax.fori_loop(0, tl, body, 0, unroll=8)

    @pl.when(i == 0)
    def _():
        start_gather(0, 0)

    @pl.when(i + 1 < n)
    def _():
        start_gather(i + 1, 1 - slot)

    pltpu.make_async_copy(ys_hbm.at[pl.ds(0, tl * SUBLANES), :], ybuf.at[slot], gsem.at[slot]).wait()
    moe = _load_rows_linear(ybuf.at[slot], tl, D_MODEL // LANES)
    y = _ln(ALPHA * h1_ref[...] + moe, ln2g_ref[...], ln2b_ref[...])
    gate = _sigmoid(_dot(y.astype(BF16), wpg_ref[...]))
    out_ref[...] = y + gate * _dot(p_ref[...].astype(BF16), wple_ref[...])


def _combine_call(dest, ys, h1, p, ln2g, ln2b, wple, wpg, *, tl):
    n_tok = h1.shape[0]
    assert n_tok % tl == 0 and tl % LANES == 0
    const2 = lambda i, d: (0, 0)
    grid_spec = pltpu.PrefetchScalarGridSpec(
        num_scalar_prefetch=1,
        grid=(n_tok // tl,),
        in_specs=[
            pl.BlockSpec(memory_space=pl.ANY),
            pl.BlockSpec((tl, D_MODEL), lambda i, d: (i, 0)),
            pl.BlockSpec((tl, PLE_DIM), lambda i, d: (i, 0)),
            pl.BlockSpec((1, D_MODEL), const2), pl.BlockSpec((1, D_MODEL), const2),
            pl.BlockSpec((PLE_DIM, D_MODEL), const2),
            pl.BlockSpec((D_MODEL, D_MODEL), const2),
        ],
        out_specs=pl.BlockSpec((tl, D_MODEL), lambda i, d: (i, 0)),
        scratch_shapes=[
            pltpu.VMEM((2, tl * SUBLANES, LANES), F32),
            pltpu.SemaphoreType.DMA((2,)),
        ],
    )
    return pl.pallas_call(
        functools.partial(_combine_kernel, tl=tl),
        grid_spec=grid_spec,
        out_shape=jax.ShapeDtypeStruct((n_tok, D_MODEL), F32),
        compiler_params=pltpu.CompilerParams(
            dimension_semantics=("arbitrary",),
            vmem_limit_bytes=_VMEM_LIMIT),
        name="combine",
    )(dest, ys, h1, p, ln2g, ln2b, wple, wpg)


def _pair_tables():
    lo, hi = np.triu_indices(EXPERTS_PER_GROUP, k=1)
    e1 = (np.arange(N_GROUPS)[:, None] * EXPERTS_PER_GROUP + lo[None, :]).reshape(-1)
    e2 = (np.arange(N_GROUPS)[:, None] * EXPERTS_PER_GROUP + hi[None, :]).reshape(-1)
    return e1.astype(np.int32), e2.astype(np.int32)


def _forward(x_prompt, x_sample, p_prompt, p_sample, state_conv, state_gla, ln_in_g, ln_in_b, w_in, w_gate_up,
             b_gate, conv_w, conv_b, conv_ln_g, conv_ln_b, w_conv_out, gla_norm_g, w_o, ln1_g, ln1_b, w_rg, b_rg,
             w_re, b_re, w_e_gate, w_e_up, w_e_down, ln2_g, ln2_b, w_ple, w_ple_gate, *, tl, tm, tl_out):
    bp, lp, _ = x_prompt.shape
    bs, ls, _ = x_sample.shape
    n_p, n_s = bp * lp, bs * ls
    n_rows = n_p + n_s
    row = lambda a: a.reshape(1, -1).astype(F32)

    w_in0 = w_in[0]
    wz = jnp.pad(w_in0[:, N_MAIN:], ((0, 0), (0, LANES - GATE_RANK))).astype(BF16)
    wgu = jnp.pad(w_gate_up[0], ((0, LANES - GATE_RANK), (0, 0))).astype(BF16)
    wr = jnp.pad(jnp.concatenate([w_rg[0], w_re[0]], axis=1),
                 ((0, 0), (0, LANES - N_GROUPS - N_EXPERTS))).astype(F32)
    br = jnp.pad(jnp.concatenate([b_rg[0], b_re[0]]), (0, LANES - N_GROUPS - N_EXPERTS)).reshape(1, LANES)
    wts = [
        row(ln_in_g), row(ln_in_b),
        w_in0[:, :N_MAIN].astype(BF16), wz, wgu, row(b_gate[0]),
        jnp.repeat(conv_w[0], CONV_RB, axis=0), row(conv_b[0]), row(conv_ln_g[0]), row(conv_ln_b[0]),
        w_conv_out[0].astype(BF16), row(gla_norm_g[0]), w_o[0].astype(BF16), row(ln1_g[0]), row(ln1_b[0]),
        wr, br,
    ]

    n_pages = (n_rows + N_BUCKETS * (tm - 1)) // tm + 1
    npl = -(-n_pages // LANES) * LANES
    xs = jnp.zeros((n_pages * tm * SUBLANES, LANES), I32)
    st = jnp.zeros((LANES, LANES), I32)
    tb = jnp.zeros((SUBLANES, npl), I32)
    zc = jnp.zeros((bp, HIST, D_CONV), F32)
    zs = jnp.zeros((bp, N_HEADS, HEAD_V, HEAD_K), F32)
    h1_p, dest_p, conv_p, gla_p, st, tb, xs = _mixer_call(
        x_prompt, zc, zs, st, tb, wts, xs, nb=1, tls=tl, tm=tm, npl=npl)
    sc = jnp.pad(state_conv[0], ((0, 0), (HIST - CONV_STATE, 0), (0, 0)))
    sg = jnp.swapaxes(state_gla[0], -1, -2)
    h1_s, dest_s, conv_s, gla_s, st, tb, xs = _mixer_call(
        x_sample, sc, sg, st, tb, wts, xs, nb=bs, tls=ls, tm=tm, npl=npl)

    pe1, pe2 = _pair_tables()
    nt = st[0, 2]
    page_bucket = tb[0, :n_pages]
    last = page_bucket[jnp.maximum(nt - 1, 0)]
    page_bucket = jnp.where(jnp.arange(n_pages) < nt, page_bucket, last)
    te1 = jnp.asarray(pe1)[page_bucket]
    te2 = jnp.asarray(pe2)[page_bucket]
    ys = _expert_call(te1, te2, nt.reshape(1), xs,
                      w_e_gate[0].astype(BF16), w_e_up[0].astype(BF16), w_e_down[0].astype(BF16),
                      tm=tm, n_pages=n_pages)

    wple = w_ple[0].astype(BF16)
    wpg = w_ple_gate[0].astype(BF16)
    ln2 = (row(ln2_g[0]), row(ln2_b[0]))
    y_p = _combine_call(dest_p[:, 0, :].reshape(-1, LANES), ys, h1_p, p_prompt[0].reshape(n_p, PLE_DIM),
                        *ln2, wple, wpg, tl=tl_out)
    y_s = _combine_call(dest_s[:, 0, :].reshape(-1, LANES), ys, h1_s, p_sample[0].reshape(n_s, PLE_DIM),
                        *ln2, wple, wpg, tl=n_s)

    fix_conv = lambda c: c[:, HIST - CONV_STATE:, :][None]
    fix_gla = lambda s: jnp.swapaxes(s, -1, -2)[None]
    return (y_p.reshape(bp, lp, D_MODEL), y_s.reshape(bs, ls, D_MODEL),
            fix_conv(conv_p), fix_gla(gla_p), fix_conv(conv_s), fix_gla(gla_s))


def kernel(x_prompt, x_sample, p_prompt, p_sample, state_conv, state_gla, ln_in_g, ln_in_b, w_in, w_gate_up, b_gate, conv_w, conv_b, conv_ln_g, conv_ln_b, w_conv_out, gla_norm_g, w_o, ln1_g, ln1_b, w_rg, b_rg, w_re, b_re, w_e_gate, w_e_up, w_e_down, ln2_g, ln2_b, w_ple, w_ple_gate):
    return _forward(x_prompt, x_sample, p_prompt, p_sample, state_conv, state_gla, ln_in_g, ln_in_b, w_in,
                    w_gate_up, b_gate, conv_w, conv_b, conv_ln_g, conv_ln_b, w_conv_out, gla_norm_g, w_o, ln1_g,
                    ln1_b, w_rg, b_rg, w_re, b_re, w_e_gate, w_e_up, w_e_down, ln2_g, ln2_b, w_ple, w_ple_gate,
                    tl=256, tm=256, tl_out=256)
```

```python
import functools

import numpy as np
import jax
import jax.numpy as jnp
from jax import lax
from jax.experimental import pallas as pl
from jax.experimental.pallas import tpu as pltpu

D_MODEL = 1024
D_CONV = 1024
CONV_WIDTH = 31
CONV_STATE = CONV_WIDTH - 1
N_HEADS = 4
KEY_DIM = 512
VAL_DIM = 1024
HEAD_K = 128
HEAD_V = 256
GATE_RANK = 16
GATE_NORM = 16.0
CHUNK = 64
N_GROUPS = 4
EXPERTS_PER_GROUP = 8
N_EXPERTS = 32
D_EXPERT = 512
PLE_DIM = 256
LN_EPS = 1e-5
RMS_EPS = 1e-6
ALPHA = 2.0 ** 0.25
LANES = 128
SUBLANES = 8
HIST = 4 * SUBLANES
N_PAIRS = EXPERTS_PER_GROUP * (EXPERTS_PER_GROUP - 1) // 2
N_BUCKETS = N_GROUPS * N_PAIRS
N_MAIN = 2 * D_CONV + 2 * KEY_DIM + 2 * VAL_DIM + 2 * D_MODEL
HALF = D_MODEL // 2
W_SUB = HALF // LANES
_VMEM_LIMIT = 60 * 1024 * 1024

O_CVA, O_CVB, O_Q, O_K, O_V, O_G, O_GA, O_GB = 0, 1024, 2048, 2560, 3072, 4096, 5120, 6144

F32 = jnp.float32
BF16 = jnp.bfloat16
I32 = jnp.int32


def _ln(x, g, b):
    mu = jnp.mean(x, axis=-1, keepdims=True)
    xc = x - mu
    var = jnp.mean(xc * xc, axis=-1, keepdims=True)
    return xc * lax.rsqrt(var + LN_EPS) * g + b


def _sigmoid(x):
    return 1.0 / (1.0 + jnp.exp(-x))


def _silu(x):
    return x * _sigmoid(x)


def _log_sigmoid(x):
    return jnp.minimum(x, 0.0) - jnp.log1p(jnp.exp(-jnp.abs(x)))


def _dot(a, b):
    return jnp.dot(a, b, preferred_element_type=F32)


def _dot_nt(a, b):
    return lax.dot_general(a, b, (((1,), (1,)), ((), ())), preferred_element_type=F32)


def _store_rows_linear(ref, val, n_rows):
    for ct in range(val.shape[1] // LANES):
        ref[pl.ds(ct, n_rows, stride=SUBLANES), :] = val[:, ct * LANES:(ct + 1) * LANES]


def _load_rows_linear(ref, n_rows, n_sub):
    return jnp.concatenate(
        [ref[pl.ds(ct, n_rows, stride=SUBLANES), :] for ct in range(n_sub)], axis=-1)


def _log2(n):
    assert n & (n - 1) == 0
    return n.bit_length() - 1


def _route(h1, h1b, w3_ref, br_ref):
    tl = h1.shape[0]
    h_lo = (h1 - h1b.astype(F32)).astype(BF16)
    logits = _dot(jnp.concatenate([h1b, h_lo, h1b], axis=1), w3_ref[...]) + br_ref[...]
    lane = lax.broadcasted_iota(I32, (tl, LANES), 1)
    neg = -jnp.inf
    gl = jnp.where(lane < N_GROUPS, logits, neg)
    gmax = jnp.max(gl, axis=-1, keepdims=True)
    grp = jnp.min(jnp.where(gl == gmax, lane, LANES), axis=-1, keepdims=True)
    g_w = 1.0 / jnp.sum(jnp.exp(gl - gmax), axis=-1, keepdims=True)
    e0 = N_GROUPS + grp * EXPERTS_PER_GROUP
    el = jnp.where((lane >= e0) & (lane < e0 + EXPERTS_PER_GROUP), logits, neg)
    v1 = jnp.max(el, axis=-1, keepdims=True)
    i1 = jnp.min(jnp.where(el == v1, lane, LANES), axis=-1, keepdims=True)
    el2 = jnp.where(lane == i1, neg, el)
    v2 = jnp.max(el2, axis=-1, keepdims=True)
    i2 = jnp.min(jnp.where(el2 == v2, lane, LANES), axis=-1, keepdims=True)
    e21 = jnp.exp(v2 - v1)
    wt1 = g_w * (1.0 / (1.0 + e21))
    wt2 = g_w * (e21 / (1.0 + e21))
    first_lo = i1 < i2
    lo = jnp.minimum(i1, i2) - e0
    hi = jnp.maximum(i1, i2) - e0
    bucket = grp * N_PAIRS + ((lo * (2 * EXPERTS_PER_GROUP - 1 - lo)) >> 1) + (hi - lo - 1)
    return bucket, jnp.where(first_lo, wt1, wt2), jnp.where(first_lo, wt2, wt1)


def _assign_slots(bucket, st_ref, tb_ref, *, tm, npl):
    tl = bucket.shape[0]
    sh = _log2(tm)
    bt = jnp.broadcast_to(bucket.astype(F32), (tl, LANES)).T
    sub = lax.broadcasted_iota(I32, (LANES, tl), 0)
    oh = sub == bt.astype(I32)
    ohf = oh.astype(F32)
    c_b = jnp.sum(ohf, axis=1, keepdims=True).astype(I32)
    before = lax.broadcasted_iota(I32, (tl, tl), 0) < lax.broadcasted_iota(I32, (tl, tl), 1)
    cum = _dot(ohf.astype(BF16), before.astype(BF16))
    rank = jnp.sum(ohf * cum, axis=0, keepdims=True).astype(I32)

    n_b = st_ref[:, 0:1]
    cur_b = st_ref[:, 1:2]
    nxt = st_ref[:, 2:3]
    part_b = (n_b & (tm - 1)) > 0
    full_b = n_b >> sh
    ceil_b = full_b + part_b.astype(I32)
    pnew_b = ((n_b + c_b + (tm - 1)) >> sh) - ceil_b
    below = lax.broadcasted_iota(I32, (LANES, LANES), 1) < lax.broadcasted_iota(I32, (LANES, LANES), 0)
    pn_f = jnp.broadcast_to(pnew_b.astype(F32), (LANES, LANES)).astype(BF16)
    base_b = nxt + _dot(below.astype(BF16), pn_f)[:, 0:1].astype(I32)
    total_new = jnp.sum(pnew_b.astype(F32), axis=0, keepdims=True).astype(I32)

    def per_token(col):
        return jnp.sum(jnp.where(oh, col.astype(F32), 0.0), axis=0, keepdims=True).astype(I32)

    n_t = per_token(n_b)
    g = n_t + rank
    q = g >> sh
    in_part = (q == (n_t >> sh)) & ((n_t & (tm - 1)) > 0)
    ceil_t = (n_t >> sh) + ((n_t & (tm - 1)) > 0).astype(I32)
    page = jnp.where(in_part, per_token(cur_b), per_token(base_b) + q - ceil_t)
    dest = page * tm + (g & (tm - 1))

    pg = lax.broadcasted_iota(I32, (LANES, npl), 1)
    owner = jnp.where((pg >= base_b) & (pg < base_b + pnew_b), lax.broadcasted_iota(I32, (LANES, npl), 0), 0)
    tb_ref[...] = tb_ref[...] + jnp.broadcast_to(
        jnp.sum(owner.astype(F32), axis=0, keepdims=True).astype(I32), (SUBLANES, npl))
    lane = lax.broadcasted_iota(I32, (LANES, LANES), 1)
    st_ref[...] = jnp.where(lane == 0, n_b + c_b,
                            jnp.where(lane == 1, jnp.where(pnew_b > 0, base_b + pnew_b - 1, cur_b),
                                      jnp.where(lane == 2, nxt + total_new, 0)))
    return dest


def _conv_seq(ubuf, ph_ref, convw_ref, convb_ref, cv_ref, s, tls):
    n_ph = tls + HIST - SUBLANES
    for p in range(1, SUBLANES):
        ph_ref[p - 1] = ubuf[s, p:p + n_ph, :]
    for ct in range(D_CONV // LANES):
        lanes = slice(ct * LANES, (ct + 1) * LANES)
        w = [convw_ref[j * SUBLANES:(j + 1) * SUBLANES, lanes] for j in range(CONV_WIDTH)]
        bias = jnp.broadcast_to(convb_ref[:, lanes], (SUBLANES, LANES))
        for rb in range(tls // SUBLANES):
            r0 = rb * SUBLANES
            acc = bias
            for j in range(CONV_WIDTH):
                shift = HIST - CONV_STATE + j
                p, a = shift % SUBLANES, shift - shift % SUBLANES
                rows = slice(r0 + a, r0 + a + SUBLANES)
                acc = acc + w[j] * (ubuf[s, rows, lanes] if p == 0 else ph_ref[p - 1, rows, lanes])
            cv_ref[s * tls + r0:s * tls + r0 + SUBLANES, lanes] = acc


def _mixer_kernel(x_ref, cst_ref, gst_ref, st_in_ref, tb_in_ref, lng_ref, lnb_ref, bg_ref, convw_ref, convb_ref,
                  clng_ref, clnb_ref, gng_ref, ln1g_ref, ln1b_ref, br_ref,
                  w_in_hbm, wz_hbm, wgu_hbm, wco_hbm, wo_hbm, w3_hbm, xs_in_hbm,
                  h1_ref, dest_ref, cst_out_ref, gst_out_ref, st_out_ref, tb_out_ref, xs_hbm,
                  w_in_ref, wz_ref, wgu_ref, wco_ref, wo_ref, w3_ref,
                  ubuf, ph_ref, cv_ref, s_ref, o_ref, xrow, dvm, dsm, st_s, tb_s, ssem, dsem, wsem,
                  *, nb, tls, chunk, tm, dump_row):
    del xs_in_hbm
    tl = nb * tls
    npl = tb_s.shape[1]
    step = pl.program_id(0) * pl.num_programs(1) + pl.program_id(1)
    n_steps = pl.num_programs(0) * pl.num_programs(1)
    slot = step % 2
    to_smem = pltpu.make_async_copy(dvm, dsm, dsem.at[0])

    def issue_row(j, sl):
        d = dsm[0, j]
        pltpu.make_async_copy(
            xrow.at[sl, pl.ds(pl.multiple_of(j * SUBLANES, SUBLANES), SUBLANES), :],
            xs_hbm.at[pl.ds(pl.multiple_of(d * SUBLANES, SUBLANES), SUBLANES), :],
            ssem.at[sl]).start()

    def scatter_done(sl):
        pltpu.make_async_copy(xrow.at[sl], xs_hbm.at[pl.ds(0, tl * SUBLANES), :], ssem.at[sl]).wait()

    @pl.when(step == 0)
    def _():
        copies = [pltpu.make_async_copy(src, dst, wsem.at[n]) for n, (src, dst) in enumerate(
            ((w_in_hbm, w_in_ref), (wz_hbm, wz_ref), (wgu_hbm, wgu_ref), (wco_hbm, wco_ref), (wo_hbm, wo_ref),
             (w3_hbm, w3_ref)))]
        for c in copies:
            c.start()
        st_s[...] = st_in_ref[...]
        tb_s[...] = tb_in_ref[...]
        xrow[...] = jnp.zeros(xrow.shape, I32)

        def prime(j, carry):
            dsm[0, j] = dump_row + j
            return carry
        lax.fori_loop(0, tl, prime, 0)
        for c in copies:
            c.wait()

    @pl.when(step >= 1)
    def _():
        to_smem.wait()

    for j in range(tl):
        issue_row(j, 1 - slot)

    @pl.when(pl.program_id(1) == 0)
    def _():
        ubuf[:, 0:HIST, :] = cst_ref[...]
        s_ref[...] = gst_ref[...]

    x = x_ref[...].reshape(tl, D_MODEL)
    h = _ln(x, lng_ref[...], lnb_ref[...])
    hb = h.astype(BF16)

    def proj(lo, width):
        return _dot(hb, w_in_ref[:, lo:lo + width])

    u = proj(O_CVA, D_CONV) * _sigmoid(proj(O_CVB, D_CONV))
    ubuf[:, HIST:HIST + tls, :] = u.reshape(nb, tls, D_CONV)
    for s in range(nb):
        _conv_seq(ubuf, ph_ref, convw_ref, convb_ref, cv_ref, s, tls)
    tail = ubuf[:, tls:tls + HIST, :]
    cst_out_ref[...] = tail
    ubuf[:, 0:HIST, :] = tail
    act = _silu(_ln(cv_ref[...], clng_ref[...], clnb_ref[...]))
    merged = _sigmoid(proj(O_GA, D_MODEL)) * _dot(act.astype(BF16), wco_ref[...])

    q = proj(O_Q, KEY_DIM) * (HEAD_K ** -0.5)
    k = proj(O_K, KEY_DIM)
    v = proj(O_V, VAL_DIM)
    z = _dot(hb, wz_ref[...])
    la = _log_sigmoid(_dot(z.astype(BF16), wgu_ref[...]) + bg_ref[...]) * (1.0 / GATE_NORM)
    row = lax.broadcasted_iota(I32, (chunk, chunk), 0)
    col = lax.broadcasted_iota(I32, (chunk, chunk), 1)
    causal = row >= col
    tri = causal.astype(F32)
    for s in range(nb):
        for c in range(tls // chunk):
            r0 = s * tls + c * chunk
            bc = jnp.dot(tri, la[r0:r0 + chunk, :], preferred_element_type=F32,
                         precision=lax.Precision.HIGHEST)
            b_mid = bc[chunk // 2:chunk // 2 + 1, :]
            b_last = bc[chunk - 1:chunk, :]
            qc = q[r0:r0 + chunk, :]
            kc = k[r0:r0 + chunk, :]
            qe = (qc * jnp.exp(bc - b_mid)).astype(BF16)
            ke = (kc * jnp.exp(b_mid - bc)).astype(BF16)
            q_in = (qc * jnp.exp(bc)).astype(BF16)
            k_out = (kc * jnp.exp(b_last - bc)).astype(BF16)
            decay = jnp.exp(b_last)
            for hh in range(N_HEADS):
                k0, v0 = hh * HEAD_K, hh * HEAD_V
                vh = v[r0:r0 + chunk, v0:v0 + HEAD_V]
                scores = _dot_nt(qe[:, k0:k0 + HEAD_K], ke[:, k0:k0 + HEAD_K])
                scores = jnp.where(causal, scores, 0.0)
                st = s_ref[s, hh]
                o = (_dot(scores.astype(BF16), vh.astype(BF16))
                     + _dot_nt(q_in[:, k0:k0 + HEAD_K], st.astype(BF16)))
                s_ref[s, hh] = (st * decay[:, k0:k0 + HEAD_K]
                                + _dot(vh.T.astype(BF16), k_out[:, k0:k0 + HEAD_K]))
                o = o * lax.rsqrt(jnp.mean(o * o, axis=-1, keepdims=True) + RMS_EPS) * gng_ref[...]
                o_ref[r0:r0 + chunk, v0:v0 + HEAD_V] = o
    gst_out_ref[...] = s_ref[...]
    merged = merged + _sigmoid(proj(O_GB, D_MODEL)) * (o_ref[...] * _silu(proj(O_G, VAL_DIM)))

    h1 = _ln(ALPHA * h + _dot(merged.astype(BF16), wo_ref[...]), ln1g_ref[...], ln1b_ref[...])
    h1_ref[...] = h1
    h1b = h1.astype(BF16)

    bucket, w_lo, w_hi = _route(h1, h1b, w3_ref, br_ref)
    dest = _assign_slots(bucket, st_s, tb_s, tm=tm, npl=npl)
    st_out_ref[...] = st_s[...]
    tb_out_ref[...] = tb_s[...]
    dest8 = jnp.broadcast_to(dest, (SUBLANES, tl))
    dest_ref[0] = dest8

    @pl.when(step >= 1)
    def _():
        scatter_done(slot)

    hbits = lax.bitcast_convert_type(h1b.astype(F32), I32)
    packed = hbits[:, :HALF] | lax.shift_right_logical(hbits[:, HALF:], 16)
    _store_rows_linear(xrow.at[slot], packed, tl)
    lane = lax.broadcasted_iota(I32, (tl, LANES), 1)
    wrow = jnp.where(lane == 0, w_lo, jnp.where(lane == 1, w_hi, 0.0))
    xrow[slot, pl.ds(W_SUB, tl, stride=SUBLANES), :] = lax.bitcast_convert_type(wrow, I32)
    dvm[...] = dest8
    to_smem.start()

    @pl.when(step == n_steps - 1)
    def _():
        to_smem.wait()

        def issue(j, carry):
            issue_row(j, slot)
            return carry
        lax.fori_loop(0, tl, issue, 0, unroll=8)
        scatter_done(1 - slot)
        scatter_done(slot)


def _mixer_call(x, cst, gst, st, tb, vecs, mats, xs, *, nb, tls, tm, dump_row):
    bsz, seq, _ = x.shape
    assert bsz % nb == 0 and seq % tls == 0 and tls % SUBLANES == 0
    chunk = min(CHUNK, seq)
    assert tls % chunk == 0
    tl = nb * tls
    assert tl % LANES == 0
    nl = seq // tls
    n_steps = (bsz // nb) * nl
    npl = tb.shape[1]
    const2 = lambda b, l: (0, 0)
    full = lambda a: pl.BlockSpec(a.shape, const2)
    hbm = pl.BlockSpec(memory_space=pl.ANY)
    in_specs = [
        pl.BlockSpec((nb, tls, D_MODEL), lambda b, l: (b, l, 0)),
        pl.BlockSpec((nb, HIST, D_CONV), lambda b, l: (b, 0, 0)),
        pl.BlockSpec((nb, N_HEADS, HEAD_V, HEAD_K), lambda b, l: (b, 0, 0, 0)),
        full(st), full(tb),
    ] + [full(a) for a in vecs] + [hbm] * (len(mats) + 1)
    args = [x, cst, gst, st, tb] + list(vecs) + list(mats) + [xs]
    out_shape = (
        jax.ShapeDtypeStruct((bsz * seq, D_MODEL), F32),
        jax.ShapeDtypeStruct((n_steps, SUBLANES, tl), I32),
        jax.ShapeDtypeStruct((bsz, HIST, D_CONV), F32),
        jax.ShapeDtypeStruct((bsz, N_HEADS, HEAD_V, HEAD_K), F32),
        jax.ShapeDtypeStruct(st.shape, I32),
        jax.ShapeDtypeStruct(tb.shape, I32),
        jax.ShapeDtypeStruct(xs.shape, xs.dtype),
    )
    out_specs = (
        pl.BlockSpec((tl, D_MODEL), lambda b, l: (b * nl + l, 0)),
        pl.BlockSpec((1, SUBLANES, tl), lambda b, l: (b * nl + l, 0, 0)),
        pl.BlockSpec((nb, HIST, D_CONV), lambda b, l: (b, 0, 0)),
        pl.BlockSpec((nb, N_HEADS, HEAD_V, HEAD_K), lambda b, l: (b, 0, 0, 0)),
        full(st), full(tb),
        hbm,
    )
    return pl.pallas_call(
        functools.partial(_mixer_kernel, nb=nb, tls=tls, chunk=chunk, tm=tm, dump_row=dump_row),
        grid=(bsz // nb, nl),
        in_specs=in_specs,
        out_specs=out_specs,
        out_shape=out_shape,
        scratch_shapes=[pltpu.VMEM(m.shape, m.dtype) for m in mats] + [
            pltpu.VMEM((nb, HIST + tls, D_CONV), F32),
            pltpu.VMEM((SUBLANES - 1, tls + HIST - SUBLANES, D_CONV), F32),
            pltpu.VMEM((tl, D_CONV), F32),
            pltpu.VMEM((nb, N_HEADS, HEAD_V, HEAD_K), F32),
            pltpu.VMEM((tl, VAL_DIM), F32),
            pltpu.VMEM((2, tl * SUBLANES, LANES), I32),
            pltpu.VMEM((SUBLANES, tl), I32),
            pltpu.SMEM((SUBLANES, tl), I32),
            pltpu.VMEM(st.shape, I32),
            pltpu.VMEM(tb.shape, I32),
            pltpu.SemaphoreType.DMA((2,)),
            pltpu.SemaphoreType.DMA((1,)),
            pltpu.SemaphoreType.DMA((len(mats),)),
        ],
        input_output_aliases={len(args) - 1: 6},
        compiler_params=pltpu.CompilerParams(
            dimension_semantics=("arbitrary", "arbitrary"),
            vmem_limit_bytes=_VMEM_LIMIT),
        name="mixer",
    )(*args)


def _expert_kernel(pg_ref, te1_ref, te2_ref, nt_ref, xs_ref, wg1_ref, wu1_ref, wd1_ref, wg2_ref, wu2_ref, wd2_ref,
                   ys_ref, *, tm):
    del pg_ref, te1_ref, te2_ref

    @pl.when(pl.program_id(0) < nt_ref[0])
    def _():
        words = _load_rows_linear(xs_ref, tm, W_SUB)
        hi = lax.bitcast_convert_type(words & jnp.int32(-65536), F32)
        lo = lax.bitcast_convert_type(lax.shift_left(words, 16), F32)
        xb = jnp.concatenate([hi, lo], axis=-1).astype(BF16)
        wrow = lax.bitcast_convert_type(xs_ref[pl.ds(W_SUB, tm, stride=SUBLANES), :], F32)

        def expert(wg_ref, wu_ref, wd_ref):
            hg = _dot(xb, wg_ref[0])
            hu = _dot(xb, wu_ref[0])
            return _dot((_silu(hg) * hu).astype(BF16), wd_ref[0])

        moe = wrow[:, 0:1] * expert(wg1_ref, wu1_ref, wd1_ref) + wrow[:, 1:2] * expert(wg2_ref, wu2_ref, wd2_ref)
        _store_rows_linear(ys_ref, moe, tm)


def _expert_call(pg, te1, te2, nt, xs, wg, wu, wd, *, tm, n_pages):
    page = lambda i, pg, te1, te2, nt: (pg[i], 0)
    wspec = lambda which, r, c: pl.BlockSpec(
        (1, r, c), (lambda i, pg, te1, te2, nt: (te1[i], 0, 0)) if which == 1 else
        (lambda i, pg, te1, te2, nt: (te2[i], 0, 0)))
    grid_spec = pltpu.PrefetchScalarGridSpec(
        num_scalar_prefetch=4,
        grid=(n_pages,),
        in_specs=[
            pl.BlockSpec((tm * SUBLANES, LANES), page),
            wspec(1, D_MODEL, D_EXPERT), wspec(1, D_MODEL, D_EXPERT), wspec(1, D_EXPERT, D_MODEL),
            wspec(2, D_MODEL, D_EXPERT), wspec(2, D_MODEL, D_EXPERT), wspec(2, D_EXPERT, D_MODEL),
        ],
        out_specs=pl.BlockSpec((tm * SUBLANES, LANES), page),
    )
    return pl.pallas_call(
        functools.partial(_expert_kernel, tm=tm),
        grid_spec=grid_spec,
        out_shape=jax.ShapeDtypeStruct((n_pages * tm * SUBLANES, LANES), F32),
        compiler_params=pltpu.CompilerParams(
            dimension_semantics=("arbitrary",),
            vmem_limit_bytes=_VMEM_LIMIT),
        name="experts",
    )(pg, te1, te2, nt, xs, wg, wu, wd, wg, wu, wd)


def _combine_kernel(dest_ref, ys_hbm, h1_ref, p_ref, ln2g_ref, ln2b_ref, wple_ref, wpg_ref, out_ref,
                    ybuf, gsem, *, tl):
    i = pl.program_id(0)
    n = pl.num_programs(0)
    slot = i % 2

    def gather_row(base, j, sl):
        d = dest_ref[base + j]
        pltpu.make_async_copy(
            ys_hbm.at[pl.ds(pl.multiple_of(d * SUBLANES, SUBLANES), SUBLANES), :],
            ybuf.at[sl, pl.ds(pl.multiple_of(j * SUBLANES, SUBLANES), SUBLANES), :],
            gsem.at[sl]).start()

    def gather_done(sl):
        pltpu.make_async_copy(ys_hbm.at[pl.ds(0, tl * SUBLANES), :], ybuf.at[sl], gsem.at[sl]).wait()

    @pl.when(i == 0)
    def _():
        def body(j, carry):
            gather_row(0, j, 0)
            return carry
        lax.fori_loop(0, tl, body, 0, unroll=8)

    nxt = jnp.minimum(i + 1, n - 1) * tl
    for j in range(tl):
        gather_row(nxt, j, 1 - slot)

    gather_done(slot)
    moe = _load_rows_linear(ybuf.at[slot], tl, D_MODEL // LANES)
    y = _ln(ALPHA * h1_ref[...] + moe, ln2g_ref[...], ln2b_ref[...])
    gate = _sigmoid(_dot(y.astype(BF16), wpg_ref[...]))
    out_ref[...] = y + gate * _dot(p_ref[...].astype(BF16), wple_ref[...])

    @pl.when(i == n - 1)
    def _():
        gather_done(1 - slot)


def _combine_call(dest, ys, h1, p, ln2g, ln2b, wple, wpg, *, tl):
    n_tok = h1.shape[0]
    assert n_tok % tl == 0 and tl % LANES == 0
    const2 = lambda i, d: (0, 0)
    grid_spec = pltpu.PrefetchScalarGridSpec(
        num_scalar_prefetch=1,
        grid=(n_tok // tl,),
        in_specs=[
            pl.BlockSpec(memory_space=pl.ANY),
            pl.BlockSpec((tl, D_MODEL), lambda i, d: (i, 0)),
            pl.BlockSpec((tl, PLE_DIM), lambda i, d: (i, 0)),
            pl.BlockSpec((1, D_MODEL), const2), pl.BlockSpec((1, D_MODEL), const2),
            pl.BlockSpec((PLE_DIM, D_MODEL), const2),
            pl.BlockSpec((D_MODEL, D_MODEL), const2),
        ],
        out_specs=pl.BlockSpec((tl, D_MODEL), lambda i, d: (i, 0)),
        scratch_shapes=[
            pltpu.VMEM((2, tl * SUBLANES, LANES), F32),
            pltpu.SemaphoreType.DMA((2,)),
        ],
    )
    return pl.pallas_call(
        functools.partial(_combine_kernel, tl=tl),
        grid_spec=grid_spec,
        out_shape=jax.ShapeDtypeStruct((n_tok, D_MODEL), F32),
        compiler_params=pltpu.CompilerParams(
            dimension_semantics=("arbitrary",),
            vmem_limit_bytes=_VMEM_LIMIT),
        name="combine",
    )(dest, ys, h1, p, ln2g, ln2b, wple, wpg)


def _pair_tables():
    lo, hi = np.triu_indices(EXPERTS_PER_GROUP, k=1)
    e1 = (np.arange(N_GROUPS)[:, None] * EXPERTS_PER_GROUP + lo[None, :]).reshape(-1)
    e2 = (np.arange(N_GROUPS)[:, None] * EXPERTS_PER_GROUP + hi[None, :]).reshape(-1)
    return e1.astype(np.int32), e2.astype(np.int32)


def _forward(x_prompt, x_sample, p_prompt, p_sample, state_conv, state_gla, ln_in_g, ln_in_b, w_in, w_gate_up,
             b_gate, conv_w, conv_b, conv_ln_g, conv_ln_b, w_conv_out, gla_norm_g, w_o, ln1_g, ln1_b, w_rg, b_rg,
             w_re, b_re, w_e_gate, w_e_up, w_e_down, ln2_g, ln2_b, w_ple, w_ple_gate, *, tl, tm, tl_out):
    bp, lp, _ = x_prompt.shape
    bs, ls, _ = x_sample.shape
    n_p, n_s = bp * lp, bs * ls
    n_rows = n_p + n_s
    row = lambda a: a.reshape(1, -1).astype(F32)

    w_in0 = w_in[0]
    wz = jnp.pad(w_in0[:, N_MAIN:], ((0, 0), (0, LANES - GATE_RANK))).astype(BF16)
    wgu = jnp.pad(w_gate_up[0], ((0, LANES - GATE_RANK), (0, 0))).astype(BF16)
    wr = jnp.pad(jnp.concatenate([w_rg[0], w_re[0]], axis=1),
                 ((0, 0), (0, LANES - N_GROUPS - N_EXPERTS))).astype(F32)
    wr_hi = wr.astype(BF16)
    wr_lo = (wr - wr_hi.astype(F32)).astype(BF16)
    w3 = jnp.concatenate([wr_hi, wr_hi, wr_lo], axis=0)
    br = jnp.pad(jnp.concatenate([b_rg[0], b_re[0]]), (0, LANES - N_GROUPS - N_EXPERTS)).reshape(1, LANES)
    vecs = [
        row(ln_in_g), row(ln_in_b), row(b_gate[0]),
        jnp.repeat(conv_w[0], SUBLANES, axis=0), row(conv_b[0]), row(conv_ln_g[0]), row(conv_ln_b[0]),
        row(gla_norm_g[0]), row(ln1_g[0]), row(ln1_b[0]), br,
    ]
    mats = [w_in0[:, :N_MAIN].astype(BF16), wz, wgu, w_conv_out[0].astype(BF16), w_o[0].astype(BF16), w3]

    n_pages = (n_rows + N_BUCKETS * (tm - 1)) // tm + 1
    npl = -(-n_pages // LANES) * LANES
    dump_row = n_pages * tm
    xs = jnp.zeros(((dump_row + max(tl, n_s)) * SUBLANES, LANES), I32)
    st = jnp.zeros((LANES, LANES), I32)
    tb = jnp.zeros((SUBLANES, npl), I32)
    zc = jnp.zeros((bp, HIST, D_CONV), F32)
    zs = jnp.zeros((bp, N_HEADS, HEAD_V, HEAD_K), F32)
    h1_p, dest_p, conv_p, gla_p, st, tb, xs = _mixer_call(
        x_prompt, zc, zs, st, tb, vecs, mats, xs, nb=1, tls=tl, tm=tm, dump_row=dump_row)
    sc = jnp.pad(state_conv[0], ((0, 0), (HIST - CONV_STATE, 0), (0, 0)))
    sg = jnp.swapaxes(state_gla[0], -1, -2)
    h1_s, dest_s, conv_s, gla_s, st, tb, xs = _mixer_call(
        x_sample, sc, sg, st, tb, vecs, mats, xs, nb=bs, tls=ls, tm=tm, dump_row=dump_row)

    pe1, pe2 = _pair_tables()
    nt = st[0, 2]
    used = jnp.arange(n_pages) < nt
    pg = jnp.argsort(jnp.where(used, tb[0, :n_pages], N_BUCKETS)).astype(I32)
    pg = jnp.where(used, pg, pg[jnp.maximum(nt - 1, 0)])
    page_bucket = tb[0, :n_pages][pg]
    te1 = jnp.asarray(pe1)[page_bucket]
    te2 = jnp.asarray(pe2)[page_bucket]
    ys = _expert_call(pg, te1, te2, nt.reshape(1), xs,
                      w_e_gate[0].astype(BF16), w_e_up[0].astype(BF16), w_e_down[0].astype(BF16),
                      tm=tm, n_pages=n_pages)

    wple = w_ple[0].astype(BF16)
    wpg = w_ple_gate[0].astype(BF16)
    ln2 = (row(ln2_g[0]), row(ln2_b[0]))
    y_p = _combine_call(dest_p[:, 0, :].reshape(-1), ys, h1_p, p_prompt[0].reshape(n_p, PLE_DIM),
                        *ln2, wple, wpg, tl=tl_out)
    y_s = _combine_call(dest_s[:, 0, :].reshape(-1), ys, h1_s, p_sample[0].reshape(n_s, PLE_DIM),
                        *ln2, wple, wpg, tl=n_s)

    fix_conv = lambda c: c[:, HIST - CONV_STATE:, :][None]
    fix_gla = lambda s: jnp.swapaxes(s, -1, -2)[None]
    return (y_p.reshape(bp, lp, D_MODEL), y_s.reshape(bs, ls, D_MODEL),
            fix_conv(conv_p), fix_gla(gla_p), fix_conv(conv_s), fix_gla(gla_s))


def kernel(x_prompt, x_sample, p_prompt, p_sample, state_conv, state_gla, ln_in_g, ln_in_b, w_in, w_gate_up, b_gate, conv_w, conv_b, conv_ln_g, conv_ln_b, w_conv_out, gla_norm_g, w_o, ln1_g, ln1_b, w_rg, b_rg, w_re, b_re, w_e_gate, w_e_up, w_e_down, ln2_g, ln2_b, w_ple, w_ple_gate):
    return _forward(x_prompt, x_sample, p_prompt, p_sample, state_conv, state_gla, ln_in_g, ln_in_b, w_in,
                    w_gate_up, b_gate, conv_w, conv_b, conv_ln_g, conv_ln_b, w_conv_out, gla_norm_g, w_o, ln1_g,
                    ln1_b, w_rg, b_rg, w_re, b_re, w_e_gate, w_e_up, w_e_down, ln2_g, ln2_b, w_ple, w_ple_gate,
                    tl=256, tm=256, tl_out=256)
```
